```python
import math
import jax, jax.numpy as jnp
from jax import lax
import numpy as np

D_MODEL = 2048
BATCH = 4
SEQ = 4096
DEPTH = 1

GRID_W = 64
CTX_LEN = 256

HY_WIDTH = 2048
HY_SHORT_K = 3
HY_EMB_DIM = 33
HY_FILTER_ORDER = 64
HY_MIN_DECAY = math.log(1e-2) / 1.5
HY_MAX_DECAY = math.log(1e-2) / 0.3

GDN_HEADS = 16
GDN_DK = 128
GDN_DV = 128
GDN_SHORT_K = 3
GDN_CHUNK = 64
D_QK = GDN_HEADS * GDN_DK
D_V = GDN_HEADS * GDN_DV

D_FF = -(-8 * D_MODEL // (3 * 256)) * 256

SPLITS = (3 * HY_WIDTH, 2 * D_QK + D_V, D_V, 4 * GDN_HEADS, 2 * D_MODEL)
D_IN = sum(SPLITS)
SPLIT_IDX = [int(s) for s in np.cumsum(SPLITS)[:-1]]

EPS = 1e-6

kernel_name = "hyena_gdn_hybrid_dit_block"


def rmsnorm(x, g):
    xf = x.astype(jnp.float32)
    xf = xf * lax.rsqrt(jnp.mean(xf * xf, axis=-1, keepdims=True) + EPS)
    return xf.astype(x.dtype) * g


def l2norm(x):
    xf = x.astype(jnp.float32)
    return xf * lax.rsqrt(jnp.sum(xf * xf, axis=-1, keepdims=True) + EPS)


def short_conv(x, w):
    k_w = w.shape[0]
    pad = k_w // 2
    n = x.shape[-2]
    xp = jnp.pad(x, [(0, 0)] * (x.ndim - 2) + [(pad, pad), (0, 0)])
    return sum(xp[..., i:i + n, :] * w[i] for i in range(k_w))


def grid_conv(x, w):
    b, n, ch = x.shape
    rows = n // GRID_W
    return short_conv(x.reshape(b, rows, GRID_W, ch), w).reshape(b, n, ch)


def hyena_filters(n, fw1, fb1, fw2, fb2, fw3, fb3, fout, freq):
    f32 = jnp.float32
    t = jnp.linspace(0.0, 1.0, n, dtype=f32)[:, None]
    bands = (HY_EMB_DIM - 1) // 2
    w = 2.0 * math.pi * jnp.arange(n, dtype=f32)[:, None] / n
    f = jnp.linspace(1e-4, bands - 1, bands, dtype=f32)[None, :]
    z = jnp.concatenate([t, jnp.cos(f * w), -jnp.sin(f * w)], axis=-1)
    h = jnp.sin(freq * (z @ fw1 + fb1))
    h = jnp.sin(freq * (h @ fw2 + fb2))
    h = jnp.sin(freq * (h @ fw3 + fb3))
    h = (h @ fout).reshape(n, 2, HY_WIDTH)
    deltas = jnp.abs(jnp.linspace(HY_MIN_DECAY, HY_MAX_DECAY, HY_WIDTH, dtype=f32))
    h = h * jnp.exp(-t * deltas)[:, None, :]
    h_fwd, h_bwd = h[:, 0], h[:, 1]
    return jnp.concatenate([h_fwd, jnp.zeros((1, HY_WIDTH), h.dtype), h_bwd[:0:-1]], axis=0)


def hyena_mix(p_hy, filt, hy_bias, conv_w, conv_fn):
    u = conv_fn(p_hy, conv_w)
    x0, x1, v = jnp.split(u, 3, axis=-1)
    z = x1 * v
    n = z.shape[1]
    zf = jnp.fft.rfft(z.astype(jnp.float32), n=2 * n, axis=1)
    kf = jnp.fft.rfft(filt.astype(jnp.float32), n=2 * n, axis=0)
    y = jnp.fft.irfft(zf * kf[None], n=2 * n, axis=1)[:, :n]
    return x0 * (y.astype(z.dtype) + z * hy_bias)


def gdn_chunked(q, k, v, g, beta, s0):
    f32 = jnp.float32
    b, n, h, dk = q.shape
    dv = v.shape[-1]
    c = GDN_CHUNK
    nc = n // c

    def chunks(t):
        return jnp.moveaxis(t.astype(f32).reshape(b, nc, c, h, *t.shape[3:]), 3, 1)

    q, k, v, g, beta = (chunks(t) for t in (q, k, v, g, beta))
    q = q * (dk ** -0.5)
    g = jnp.cumsum(g, axis=-1)
    tril = jnp.tril(jnp.ones((c, c), bool))
    strict = jnp.tril(jnp.ones((c, c), bool), -1)
    decay = jnp.exp(jnp.where(tril, g[..., :, None] - g[..., None, :], -jnp.inf))
    kb = k * beta[..., None]
    a_mat = jnp.where(strict, jnp.einsum('bhncd,bhnsd->bhncs', kb, k) * decay, 0.0)
    rhs = jnp.concatenate([v * beta[..., None], kb * jnp.exp(g)[..., None]], axis=-1)
    sol = lax.linalg.triangular_solve(jnp.eye(c, dtype=f32) + a_mat, rhs,
                                      left_side=True, lower=True, unit_diagonal=True)
    u, w = sol[..., :dv], sol[..., dv:]
    qk = jnp.where(tril, jnp.einsum('bhncd,bhnsd->bhncs', q, k) * decay, 0.0)
    q_dec = q * jnp.exp(g)[..., None]
    g_last = g[..., -1]
    k_dec = k * jnp.exp(g_last[..., None] - g)[..., None]

    def step(s, xs):
        q_i, w_i, u_i, k_i, qk_i, gl_i = xs
        v_new = u_i - jnp.einsum('bhcd,bhde->bhce', w_i, s)
        o = jnp.einsum('bhcd,bhde->bhce', q_i, s) + jnp.einsum('bhcs,bhse->bhce', qk_i, v_new)
        s = s * jnp.exp(gl_i)[..., None, None] + jnp.einsum('bhcd,bhce->bhde', k_i, v_new)
        return s, o

    xs = tuple(jnp.moveaxis(t, 2, 0) for t in (q_dec, w, u, k_dec, qk, g_last))
    s_final, o = lax.scan(step, s0, xs)
    o = jnp.moveaxis(jnp.moveaxis(o, 0, 2), 1, 3).reshape(b, n, h, dv)
    return o, s_final


def gdn_core(p_qkv, p_scal, conv_fn, s_f0, s_b0, lp):
    b, n, _ = p_qkv.shape
    qkv = jax.nn.silu(conv_fn(p_qkv, lp['gdn_conv']))
    q, k, v = jnp.split(qkv, [D_QK, 2 * D_QK], axis=-1)
    q = l2norm(q.reshape(b, n, GDN_HEADS, GDN_DK))
    k = l2norm(k.reshape(b, n, GDN_HEADS, GDN_DK))
    v = v.reshape(b, n, GDN_HEADS, GDN_DV)
    s = p_scal.astype(jnp.float32).reshape(b, n, 2, 2, GDN_HEADS)
    beta = jax.nn.sigmoid(s[:, :, 0])
    g = -jnp.exp(lp['gdn_a_log']) * jax.nn.softplus(s[:, :, 1] + lp['gdn_dt_bias'])
    o_f, s_f = gdn_chunked(q, k, v, g[:, :, 0], beta[:, :, 0], s_f0)
    rev = lambda t: jnp.flip(t, axis=1)
    o_b, s_b = gdn_chunked(rev(q), rev(k), rev(v), rev(g[:, :, 1]), rev(beta[:, :, 1]), s_b0)
    return (o_f + rev(o_b)).astype(p_qkv.dtype), s_f, s_b


def mixer_out(p_hy, p_z, p_gate, o_gdn, conv_fn, lp):
    n = p_hy.shape[1]
    filt = hyena_filters(n, lp['hy_fw1'], lp['hy_fb1'], lp['hy_fw2'], lp['hy_fb2'],
                         lp['hy_fw3'], lp['hy_fb3'], lp['hy_fout'], lp['hy_freq'])
    y_hy = hyena_mix(p_hy, filt, lp['hy_bias'], lp['hy_conv'], conv_fn) @ lp['w_hy_out']
    b = o_gdn.shape[0]
    z = p_z.reshape(b, n, GDN_HEADS, GDN_DV)
    y_gdn = (rmsnorm(o_gdn, lp['gdn_norm']) * jax.nn.silu(z)).reshape(b, n, D_V) @ lp['w_gdn_out']
    g_hy, g_gdn = jnp.split(jax.nn.sigmoid(p_gate), 2, axis=-1)
    return (g_hy * y_hy + g_gdn * y_gdn) @ lp['w_o']


def swiglu(h, w_up, w_down):
    gate, up = jnp.split(h @ w_up, 2, axis=-1)
    return (jax.nn.silu(gate) * up) @ w_down


def setup_inputs(seed: int = 0) -> dict:
    key = jax.random.key(seed)
    ks = iter(jax.random.split(key, 40))
    f32 = jnp.float32

    def nrm(shape, scale):
        return jax.random.normal(next(ks), shape, f32) * scale

    def gain(shape):
        return 1.0 + nrm(shape, 0.01)

    dt = jnp.exp(jax.random.uniform(next(ks), (DEPTH, 2, GDN_HEADS), f32, math.log(1e-3), math.log(1e-1)))
    return {
        'x': nrm((BATCH, SEQ, D_MODEL), 1.0),
        'c': nrm((BATCH, D_MODEL), 1.0),
        'ctx': nrm((BATCH, CTX_LEN, D_MODEL), 1.0),
        'c_ctx': nrm((D_MODEL,), 1.0),
        'w_ada': nrm((DEPTH, D_MODEL, 6 * D_MODEL), D_MODEL ** -0.5),
        'b_ada': nrm((DEPTH, 6 * D_MODEL), 0.01),
        'norm_mix': gain((DEPTH, D_MODEL)),
        'norm_ffn': gain((DEPTH, D_MODEL)),
        'w_in': nrm((DEPTH, D_MODEL, D_IN), D_MODEL ** -0.5),
        'hy_conv': nrm((DEPTH, HY_SHORT_K, 3 * HY_WIDTH), HY_SHORT_K ** -0.5),
        'hy_bias': nrm((DEPTH, HY_WIDTH), 1.0),
        'hy_fw1': nrm((DEPTH, HY_EMB_DIM, HY_FILTER_ORDER), HY_EMB_DIM ** -0.5),
        'hy_fb1': nrm((DEPTH, HY_FILTER_ORDER), 0.1),
        'hy_fw2': nrm((DEPTH, HY_FILTER_ORDER, HY_FILTER_ORDER), HY_FILTER_ORDER ** -0.5),
        'hy_fb2': nrm((DEPTH, HY_FILTER_ORDER), 0.1),
        'hy_fw3': nrm((DEPTH, HY_FILTER_ORDER, HY_FILTER_ORDER), HY_FILTER_ORDER ** -0.5),
        'hy_fb3': nrm((DEPTH, HY_FILTER_ORDER), 0.1),
        'hy_fout': nrm((DEPTH, HY_FILTER_ORDER, 2 * HY_WIDTH), 0.02),
        'hy_freq': gain((DEPTH, HY_FILTER_ORDER)),
        'gdn_conv': nrm((DEPTH, GDN_SHORT_K, 2 * D_QK + D_V), GDN_SHORT_K ** -0.5),
        'gdn_a_log': jnp.log(jax.random.uniform(next(ks), (DEPTH, 2, GDN_HEADS), f32, 1.0, 16.0)),
        'gdn_dt_bias': dt + jnp.log(-jnp.expm1(-dt)),
        'gdn_norm': gain((DEPTH, GDN_DV)),
        'w_hy_out': nrm((DEPTH, HY_WIDTH, D_MODEL), HY_WIDTH ** -0.5),
        'w_gdn_out': nrm((DEPTH, D_V, D_MODEL), D_V ** -0.5),
        'w_o': nrm((DEPTH, D_MODEL, D_MODEL), D_MODEL ** -0.5),
        'w_up': nrm((DEPTH, D_MODEL, 2 * D_FF), D_MODEL ** -0.5),
        'w_down': nrm((DEPTH, D_FF, D_MODEL), D_FF ** -0.5),
        'norm_final': gain((D_MODEL,)),
    }


def reference(x, c, ctx, c_ctx, w_ada, b_ada, norm_mix, norm_ffn, w_in, hy_conv, hy_bias,
              hy_fw1, hy_fb1, hy_fw2, hy_fb2, hy_fw3, hy_fb3, hy_fout, hy_freq,
              gdn_conv, gdn_a_log, gdn_dt_bias, gdn_norm, w_hy_out, w_gdn_out, w_o,
              w_up, w_down, norm_final):
    b = x.shape[0]
    zero_state = jnp.zeros((b, GDN_HEADS, GDN_DK, GDN_DV), jnp.float32)
    for l in range(DEPTH):
        lp = {
            'hy_conv': hy_conv[l], 'hy_bias': hy_bias[l],
            'hy_fw1': hy_fw1[l], 'hy_fb1': hy_fb1[l], 'hy_fw2': hy_fw2[l], 'hy_fb2': hy_fb2[l],
            'hy_fw3': hy_fw3[l], 'hy_fb3': hy_fb3[l], 'hy_fout': hy_fout[l], 'hy_freq': hy_freq[l],
            'gdn_conv': gdn_conv[l], 'gdn_a_log': gdn_a_log[l], 'gdn_dt_bias': gdn_dt_bias[l],
            'gdn_norm': gdn_norm[l], 'w_hy_out': w_hy_out[l], 'w_gdn_out': w_gdn_out[l], 'w_o': w_o[l],
        }
        mod_lat = (jax.nn.silu(c) @ w_ada[l] + b_ada[l])[:, None, :]
        mod_ctx = (jax.nn.silu(c_ctx) @ w_ada[l] + b_ada[l])[None, None, :]
        sh_a, sc_a, ga_a, sh_f, sc_f, ga_f = jnp.split(mod_lat, 6, axis=-1)
        csh_a, csc_a, cga_a, csh_f, csc_f, cga_f = jnp.split(mod_ctx, 6, axis=-1)

        h_ctx = rmsnorm(ctx, norm_mix[l]) * (1.0 + csc_a) + csh_a
        h_lat = rmsnorm(x, norm_mix[l]) * (1.0 + sc_a) + sh_a
        pc = jnp.split(h_ctx @ w_in[l], SPLIT_IDX, axis=-1)
        pl = jnp.split(h_lat @ w_in[l], SPLIT_IDX, axis=-1)
        o_c, s_f, s_b = gdn_core(pc[1], pc[3], short_conv, zero_state, zero_state, lp)
        o_l, _, _ = gdn_core(pl[1], pl[3], grid_conv, s_f, s_b, lp)
        x = x + ga_a * mixer_out(pl[0], pl[2], pl[4], o_l, grid_conv, lp)

        x = x + ga_f * swiglu(rmsnorm(x, norm_ffn[l]) * (1.0 + sc_f) + sh_f, w_up[l], w_down[l])

        if l < DEPTH - 1:
            ctx = ctx + cga_a * mixer_out(pc[0], pc[2], pc[4], o_c, short_conv, lp)
            ctx = ctx + cga_f * swiglu(rmsnorm(ctx, norm_ffn[l]) * (1.0 + csc_f) + csh_f, w_up[l], w_down[l])
    return rmsnorm(x, norm_final)
```

```python
import functools
import math

import jax
import jax.numpy as jnp
import numpy as np
from jax import lax
from jax.experimental import pallas as pl
from jax.experimental.pallas import tpu as pltpu

BF = jnp.bfloat16
F32 = jnp.float32

GRID_W = 64
CHUNK = 64
SUB = 16
EPS = 1e-6
HY_MIN_DECAY = math.log(1e-2) / 1.5
HY_MAX_DECAY = math.log(1e-2) / 0.3
LANES = 128
VMEM_LIMIT = 56 * 1024 * 1024

_NT = (((1,), (1,)), ((), ()))
_TN = (((0,), (0,)), ((), ()))


def _tile(n, pref, mult):
    t = min(pref, n)
    t -= t % mult
    while t >= mult:
        if n % t == 0:
            return t
        t -= mult
    return n


def _params(*sem):
    return pltpu.CompilerParams(dimension_semantics=sem, vmem_limit_bytes=VMEM_LIMIT)


def _dot(a, b):
    return jnp.dot(a, b, preferred_element_type=F32)


def _split3(x):
    hi = x.astype(BF)
    r = x - hi.astype(F32)
    mid = r.astype(BF)
    lo = (r - mid.astype(F32)).astype(BF)
    return hi, mid, lo


def _dot_f32(a, b):
    ah, am, _ = _split3(a)
    bh, bm, _ = _split3(b)
    return _dot(ah, bh) + (_dot(ah, bm) + _dot(am, bh))


def _silu(x):
    return x * (1.0 / (1.0 + jnp.exp(-x)))


def _sigmoid(x):
    return 1.0 / (1.0 + jnp.exp(-x))


def _ada_kernel(c_ref, w_ref, b_ref, o_ref):
    o_ref[...] = _dot(c_ref[...], w_ref[...].astype(BF)) + b_ref[...]


def _ada_call(cvec, w, b):
    d, n = w.shape
    r = cvec.shape[0]
    tn = _tile(n, 1024, LANES)
    return pl.pallas_call(
        _ada_kernel,
        grid=(n // tn,),
        in_specs=[pl.BlockSpec((r, d), lambda j: (0, 0)),
                  pl.BlockSpec((d, tn), lambda j: (0, j)),
                  pl.BlockSpec((1, tn), lambda j: (0, j))],
        out_specs=pl.BlockSpec((r, tn), lambda j: (0, j)),
        out_shape=jax.ShapeDtypeStruct((r, n), F32),
        compiler_params=_params("parallel"),
    )(cvec, w, b)


def _normmod_kernel(x_ref, g_ref, sc_ref, sh_ref, o_ref):
    x = x_ref[...]
    xn = x * lax.rsqrt(jnp.mean(x * x, axis=-1, keepdims=True) + EPS)
    o_ref[...] = ((xn * g_ref[...]) * (1.0 + sc_ref[0]) + sh_ref[0]).astype(o_ref.dtype)


def _normmod_call(x2, g, mod3, sec_sh, sec_sc, rows_per_mod, mod_row0):
    t, d = x2.shape
    tm = _tile(rows_per_mod, 256, 8)
    per = rows_per_mod // tm
    mod_map = lambda sec: (lambda i: (mod_row0 + i // per, 0, sec))
    return pl.pallas_call(
        _normmod_kernel,
        grid=(t // tm,),
        in_specs=[pl.BlockSpec((tm, d), lambda i: (i, 0)),
                  pl.BlockSpec((1, d), lambda i: (0, 0)),
                  pl.BlockSpec((1, 1, d), mod_map(sec_sc)),
                  pl.BlockSpec((1, 1, d), mod_map(sec_sh))],
        out_specs=pl.BlockSpec((tm, d), lambda i: (i, 0)),
        out_shape=jax.ShapeDtypeStruct((t, d), BF),
        compiler_params=_params("parallel"),
    )(x2, g, mod3, mod3)


def _conv3(p, cw_ref, period):
    tm = p.shape[0]
    row = lax.broadcasted_iota(jnp.int32, (tm, 1), 0) % period
    prev = jnp.where(row == 0, 0.0, pltpu.roll(p, 1, 0))
    nxt = jnp.where(row == period - 1, 0.0, pltpu.roll(p, tm - 1, 0))
    return prev * cw_ref[0:1, :] + p * cw_ref[1:2, :] + nxt * cw_ref[2:3, :]


def _proj_hy_kernel(h_ref, w0_ref, w1_ref, w2_ref, c0_ref, c1_ref, c2_ref, x0_ref, z_ref, *, period):
    h = h_ref[...]
    x0 = _conv3(_dot(h, w0_ref[...]), c0_ref, period)
    x1 = _conv3(_dot(h, w1_ref[...]), c1_ref, period)
    v = _conv3(_dot(h, w2_ref[...]), c2_ref, period)
    x0_ref[...] = x0.astype(x0_ref.dtype)
    z_ref[...] = (x1 * v).astype(z_ref.dtype)


def _proj_hy_call(h, w, cw, period):
    t, d = h.shape
    wd = w.shape[1] // 3
    tm = _tile(t, 512, period)
    tn = _tile(wd, 512, LANES)
    nb = wd // tn
    wspec = lambda s: pl.BlockSpec((d, tn), lambda i, j: (0, s * nb + j))
    cspec = lambda s: pl.BlockSpec((3, tn), lambda i, j: (0, s * nb + j))
    ospec = pl.BlockSpec((tm, tn), lambda i, j: (i, j))
    return pl.pallas_call(
        functools.partial(_proj_hy_kernel, period=period),
        grid=(t // tm, nb),
        in_specs=[pl.BlockSpec((tm, d), lambda i, j: (i, 0)), wspec(0), wspec(1), wspec(2),
                  cspec(0), cspec(1), cspec(2)],
        out_specs=[ospec, ospec],
        out_shape=[jax.ShapeDtypeStruct((t, wd), BF)] * 2,
        compiler_params=_params("parallel", "arbitrary"),
    )(h, w, w, w, cw, cw, cw)


def _proj_qkv_kernel(h_ref, w_ref, c_ref, o_ref, *, period, n_norm_tiles, head_dim):
    u = _conv3(_dot(h_ref[...], w_ref[...]), c_ref, period)
    u = _silu(u)
    j = pl.program_id(1)

    @pl.when(j < n_norm_tiles)
    def _():
        for s in range(u.shape[1] // head_dim):
            blk = u[:, s * head_dim:(s + 1) * head_dim]
            inv = lax.rsqrt(jnp.sum(blk * blk, axis=-1, keepdims=True) + EPS)
            o_ref[:, s * head_dim:(s + 1) * head_dim] = (blk * inv).astype(o_ref.dtype)

    @pl.when(j >= n_norm_tiles)
    def _():
        o_ref[...] = u.astype(o_ref.dtype)


def _proj_qkv_call(h, w, cw, period, d_qk, head_dim):
    t, d = h.shape
    n = w.shape[1]
    tm = _tile(t, 512, period)
    tn = _tile(math.gcd(2 * d_qk, n), 512, head_dim)
    return pl.pallas_call(
        functools.partial(_proj_qkv_kernel, period=period, n_norm_tiles=2 * d_qk // tn, head_dim=head_dim),
        grid=(t // tm, n // tn),
        in_specs=[pl.BlockSpec((tm, d), lambda i, j: (i, 0)),
                  pl.BlockSpec((d, tn), lambda i, j: (0, j)),
                  pl.BlockSpec((3, tn), lambda i, j: (0, j))],
        out_specs=pl.BlockSpec((tm, tn), lambda i, j: (i, j)),
        out_shape=jax.ShapeDtypeStruct((t, n), BF),
        compiler_params=_params("parallel", "arbitrary"),
    )(h, w, cw)


def _proj_act_kernel(h_ref, w_ref, o_ref, *, act):
    o_ref[...] = act(_dot(h_ref[...], w_ref[...])).astype(o_ref.dtype)


def _proj_act_call(h, w, act):
    t, d = h.shape
    n = w.shape[1]
    tm = _tile(t, 512, 16)
    tn = _tile(n, 1024, LANES)
    return pl.pallas_call(
        functools.partial(_proj_act_kernel, act=act),
        grid=(t // tm, n // tn),
        in_specs=[pl.BlockSpec((tm, d), lambda i, j: (i, 0)),
                  pl.BlockSpec((d, tn), lambda i, j: (0, j))],
        out_specs=pl.BlockSpec((tm, tn), lambda i, j: (i, j)),
        out_shape=jax.ShapeDtypeStruct((t, n), BF),
        compiler_params=_params("parallel", "arbitrary"),
    )(h, w)


def _scal_kernel(h_ref, w_ref, a_ref, dt_ref, o_ref, *, hb):
    s = _dot(h_ref[...], w_ref[...])
    tm = s.shape[0]
    lane = lax.broadcasted_iota(jnp.int32, (1, s.shape[1]), 1) % LANES
    beta = _sigmoid(s)
    xs = s + dt_ref[...]
    softplus = jnp.maximum(xs, 0.0) + jnp.log(1.0 + jnp.exp(-jnp.abs(xs)))
    g = jnp.where((lane >= 2 * hb) & (lane < 4 * hb), -jnp.exp(a_ref[...]) * softplus, 0.0)
    ri = lax.broadcasted_iota(jnp.int32, (tm, tm), 0)
    ci = lax.broadcasted_iota(jnp.int32, (tm, tm), 1)
    same = (ri // CHUNK) == (ci // CHUNK)
    lo = jnp.where(same & (ci <= ri), 1.0, 0.0).astype(BF)
    up = jnp.where(same & (ci >= ri), 1.0, 0.0).astype(BF)
    g0, g1, g2 = _split3(g)
    pre = _dot(lo, g0) + (_dot(lo, g1) + _dot(lo, g2))
    suf = _dot(up, g0) + (_dot(up, g1) + _dot(up, g2))
    o_ref[...] = jnp.where(lane < 2 * hb, beta, jnp.where(lane < 3 * hb, pre, jnp.where(lane < 4 * hb, suf, 0.0)))


def _scal_call(h, w, a_vec, dt_vec, hb):
    t, d = h.shape
    n = w.shape[1]
    tm = _tile(t, 256, CHUNK)
    return pl.pallas_call(
        functools.partial(_scal_kernel, hb=hb),
        grid=(t // tm,),
        in_specs=[pl.BlockSpec((tm, d), lambda i: (i, 0)),
                  pl.BlockSpec((d, n), lambda i: (0, 0)),
                  pl.BlockSpec((1, n), lambda i: (0, 0)),
                  pl.BlockSpec((1, n), lambda i: (0, 0))],
        out_specs=pl.BlockSpec((tm, n), lambda i: (i, 0)),
        out_shape=jax.ShapeDtypeStruct((t, n), F32),
        compiler_params=_params("parallel"),
    )(h, w, a_vec, dt_vec)


def _tri_inverse(a):
    c = a.shape[0]
    ri = lax.broadcasted_iota(jnp.int32, (c, c), 0)
    ci = lax.broadcasted_iota(jnp.int32, (c, c), 1)
    eye = jnp.where(ri == ci, 1.0, 0.0)
    same = (ri // SUB) == (ci // SUB)
    x = jnp.where(same, -a, 0.0)
    p = eye + x
    steps = int(math.log2(SUB)) - 1
    for _ in range(steps):
        x16 = x.astype(BF)
        x = _dot(x16, x16)
        p = p + _dot(p.astype(BF), x.astype(BF))
    size = SUB
    while size < c:
        off = jnp.where(((ri // (2 * size)) == (ci // (2 * size))) & ((ri // size) != (ci // size)), a, 0.0)
        p16 = p.astype(BF)
        p = p - _dot(_dot(p16, off.astype(BF)).astype(BF), p16)
        size *= 2
    return p


def _gdn_chain(q, k, v, sc, sct, s, lower, col_beta, col_g, scale):
    c, dk = k.shape
    beta = sc[:, col_beta:col_beta + 1]
    gcol = sc[:, col_g:col_g + 1]
    grow = sct[col_g:col_g + 1, :]
    ri = lax.broadcasted_iota(jnp.int32, (c, c), 0)
    ci = lax.broadcasted_iota(jnp.int32, (c, c), 1)
    incl = (ri >= ci) if lower else (ri <= ci)
    strict = (ri > ci) if lower else (ri < ci)
    dec = jnp.exp(jnp.where(incl, gcol - grow, -jnp.inf))
    glast = gcol[c - 1:c, :] if lower else gcol[0:1, :]
    eg = jnp.exp(gcol)
    kf = k.astype(F32)
    kb = kf * beta
    kk = lax.dot_general(kb.astype(BF), k, _NT, preferred_element_type=F32)
    qk = lax.dot_general(q, k, _NT, preferred_element_type=F32)
    a = jnp.where(strict, kk * dec, 0.0)
    qkm = jnp.where(incl, qk * dec, 0.0) * scale
    t = _tri_inverse(a)
    rhs = jnp.concatenate([v.astype(F32) * beta, kb * eg], axis=1).astype(BF)
    sol = _dot(t.astype(BF), rhs)
    dv = v.shape[1]
    u, w = sol[:, :dv], sol[:, dv:]
    s16 = s.astype(BF)
    v_new = u - _dot(w.astype(BF), s16)
    vn16 = v_new.astype(BF)
    qdec = (q.astype(F32) * (eg * scale)).astype(BF)
    o = _dot(qdec, s16) + _dot(qkm.astype(BF), vn16)
    kdec = (kf * jnp.exp(glast - gcol)).astype(BF)
    s_new = s * jnp.exp(glast) + lax.dot_general(kdec, vn16, _TN, preferred_element_type=F32)
    return o, s_new


def _gdn_kernel(qf_ref, kf_ref, vf_ref, qb_ref, kb_ref, vb_ref, scf_ref, scb_ref, stf_ref, stb_ref,
                s0f_ref, s0b_ref, of_ref, ob_ref, sf_ref, sb_ref, st_ref, *, hb, dk, dv, scale):
    n = pl.program_id(2)

    @pl.when(n == 0)
    def _():
        st_ref[0] = s0f_ref[0]
        st_ref[1] = s0b_ref[0]

    scf, scb = scf_ref[...], scb_ref[...]
    stf, stb = stf_ref[0, 0], stb_ref[0, 0]
    for j in range(hb):
        o, s_new = _gdn_chain(qf_ref[:, j * dk:(j + 1) * dk], kf_ref[:, j * dk:(j + 1) * dk],
                              vf_ref[:, j * dv:(j + 1) * dv], scf, stf, st_ref[0, j],
                              True, j, 2 * hb + j, scale)
        of_ref[:, j * dv:(j + 1) * dv] = o.astype(of_ref.dtype)
        st_ref[0, j] = s_new
        o, s_new = _gdn_chain(qb_ref[:, j * dk:(j + 1) * dk], kb_ref[:, j * dk:(j + 1) * dk],
                              vb_ref[:, j * dv:(j + 1) * dv], scb, stb, st_ref[1, j],
                              False, hb + j, 3 * hb + j, scale)
        ob_ref[:, j * dv:(j + 1) * dv] = o.astype(ob_ref.dtype)
        st_ref[1, j] = s_new

    @pl.when(n == pl.num_programs(2) - 1)
    def _():
        sf_ref[0] = st_ref[0]
        sb_ref[0] = st_ref[1]


def _gdn_call(qkv, scal, scal_t, s0f, s0b, batch, heads, dk, dv, hb):
    t = qkv.shape[0]
    nc = t // batch // CHUNK
    g = heads // hb
    qb, vb = hb * dk, hb * dv
    fwd = lambda off: (lambda b, h, n: (b * nc + n, off + h))
    bwd = lambda off: (lambda b, h, n: (b * nc + nc - 1 - n, off + h))
    k_off, v_off = heads * dk // qb, 2 * heads * dk // vb
    st_spec = pl.BlockSpec((1, hb, dk, dv), lambda b, h, n: (b, h, 0, 0))
    in_specs = [pl.BlockSpec((CHUNK, qb), fwd(0)), pl.BlockSpec((CHUNK, qb), fwd(k_off)),
                pl.BlockSpec((CHUNK, vb), fwd(v_off)),
                pl.BlockSpec((CHUNK, qb), bwd(0)), pl.BlockSpec((CHUNK, qb), bwd(k_off)),
                pl.BlockSpec((CHUNK, vb), bwd(v_off)),
                pl.BlockSpec((CHUNK, LANES), fwd(0)), pl.BlockSpec((CHUNK, LANES), bwd(0)),
                pl.BlockSpec((1, 1, LANES, CHUNK), lambda b, h, n: (b * nc + n, h, 0, 0)),
                pl.BlockSpec((1, 1, LANES, CHUNK), lambda b, h, n: (b * nc + nc - 1 - n, h, 0, 0)),
                st_spec, st_spec]
    out_specs = [pl.BlockSpec((CHUNK, vb), fwd(0)), pl.BlockSpec((CHUNK, vb), bwd(0)), st_spec, st_spec]
    o_sds = jax.ShapeDtypeStruct((t, heads * dv), F32)
    s_sds = jax.ShapeDtypeStruct((batch, heads, dk, dv), F32)
    return pl.pallas_call(
        functools.partial(_gdn_kernel, hb=hb, dk=dk, dv=dv, scale=dk ** -0.5),
        grid=(batch, g, nc),
        in_specs=in_specs,
        out_specs=out_specs,
        out_shape=[o_sds, o_sds, s_sds, s_sds],
        scratch_shapes=[pltpu.VMEM((2, hb, dk, dv), F32)],
        compiler_params=_params("parallel", "parallel", "arbitrary"),
    )(qkv, qkv, qkv, qkv, qkv, qkv, scal, scal, scal_t, scal_t, s0f, s0b)


def _filter_kernel(z_ref, w1_ref, b1_ref, w2_ref, b2_ref, w3_ref, b3_ref, fr_ref, ff_ref, fb_ref,
                   t_ref, dl_ref, hs_ref, hd_ref, kn_ref, h3_ref):
    @pl.when(pl.program_id(0) == 0)
    def _():
        fr = fr_ref[...]
        h = jnp.sin(fr * (_dot_f32(z_ref[...], w1_ref[...]) + b1_ref[...]))
        h = jnp.sin(fr * (_dot_f32(h, w2_ref[...]) + b2_ref[...]))
        h3_ref[...] = jnp.sin(fr * (_dot_f32(h, w3_ref[...]) + b3_ref[...]))

    h3 = h3_ref[...]
    n = h3.shape[0]
    dec = jnp.exp(-t_ref[...] * dl_ref[...])
    hf = _dot_f32(h3, ff_ref[...]) * dec
    row = lax.broadcasted_iota(jnp.int32, (n, 1), 0)
    hb = jnp.where(row == 0, 0.0, _dot_f32(h3, fb_ref[...]) * dec)
    hs = hf + hb
    hs_ref[...] = hs.astype(hs_ref.dtype)
    hd_ref[...] = (hb - hf).astype(hd_ref.dtype)
    sign = (1 - 2 * (row % 2)).astype(F32)
    kn_ref[...] = jnp.sum(hs * sign, axis=0, keepdims=True)


def _filter_call(zfeat, w1, b1, w2, b2, w3, b3, freq, fout, tcol, deltas):
    n, e = zfeat.shape
    o = w1.shape[1]
    wd = fout.shape[1] // 2
    tn = _tile(wd, 512, LANES)
    nb = wd // tn
    full = lambda a: pl.BlockSpec(a.shape, lambda j: (0,) * a.ndim)
    return pl.pallas_call(
        _filter_kernel,
        grid=(nb,),
        in_specs=[full(zfeat), full(w1), full(b1), full(w2), full(b2), full(w3), full(b3), full(freq),
                  pl.BlockSpec((o, tn), lambda j: (0, j)), pl.BlockSpec((o, tn), lambda j: (0, nb + j)),
                  full(tcol), pl.BlockSpec((1, tn), lambda j: (0, j))],
        out_specs=[pl.BlockSpec((n, tn), lambda j: (0, j)), pl.BlockSpec((n, tn), lambda j: (0, j)),
                   pl.BlockSpec((1, tn), lambda j: (0, j))],
        out_shape=[jax.ShapeDtypeStruct((n, wd), BF), jax.ShapeDtypeStruct((n, wd), BF),
                   jax.ShapeDtypeStruct((1, wd), F32)],
        scratch_shapes=[pltpu.VMEM((n, o), F32)],
        compiler_params=_params("arbitrary"),
    )(zfeat, w1, b1, w2, b2, w3, b3, freq, fout, fout, tcol, deltas)


def _dftmat_kernel(ca_ref, sa_ref, cb_ref, sb_ref, m_ref):
    tk = ca_ref.shape[0]
    cb, sb = cb_ref[...], sb_ref[...]
    for t1 in range(ca_ref.shape[1]):
        ca = ca_ref[:, t1:t1 + 1]
        sa = sa_ref[:, t1:t1 + 1]
        m_ref[0:tk, t1 * LANES:(t1 + 1) * LANES] = (ca * cb - sa * sb).astype(m_ref.dtype)
        m_ref[tk:2 * tk, t1 * LANES:(t1 + 1) * LANES] = (sa * cb + ca * sb).astype(m_ref.dtype)


def _dft_tile(n):
    return _tile(n, 256, 16)


def _dftmat_call(n):
    n1 = n // LANES
    k = jnp.arange(n, dtype=jnp.int32)[:, None]
    ang_a = ((k * (LANES * jnp.arange(n1, dtype=jnp.int32)[None, :])) % (2 * n)).astype(F32) * (math.pi / n)
    ang_b = ((k * jnp.arange(LANES, dtype=jnp.int32)[None, :]) % (2 * n)).astype(F32) * (math.pi / n)
    tk = _dft_tile(n)
    a_spec = pl.BlockSpec((tk, n1), lambda i: (i, 0))
    b_spec = pl.BlockSpec((tk, LANES), lambda i: (i, 0))
    return pl.pallas_call(
        _dftmat_kernel,
        grid=(n // tk,),
        in_specs=[a_spec, a_spec, b_spec, b_spec],
        out_specs=pl.BlockSpec((2 * tk, n), lambda i: (i, 0)),
        out_shape=jax.ShapeDtypeStruct((2 * n, n), BF),
        compiler_params=_params("parallel"),
    )(jnp.cos(ang_a), jnp.sin(ang_a), jnp.cos(ang_b), jnp.sin(ang_b))


def _kf_kernel(m_ref, hs_ref, hd_ref, k_ref, *, n):
    tk = m_ref.shape[0] // 2
    k = pl.program_id(0) * tk + lax.broadcasted_iota(jnp.int32, (tk, 1), 0)
    wk = jnp.where(k == 0, 0.5 / n, 1.0 / n)
    k_ref[0:tk, :] = _dot(m_ref[0:tk, :], hs_ref[...]) * wk
    k_ref[tk:2 * tk, :] = _dot(m_ref[tk:2 * tk, :], hd_ref[...]) * wk


def _kf_call(dft, hs, hd):
    n = dft.shape[1]
    wd = hs.shape[1]
    tk = _dft_tile(n)
    tn = _tile(wd, 512, LANES)
    h_spec = pl.BlockSpec((n, tn), lambda i, j: (0, j))
    return pl.pallas_call(
        functools.partial(_kf_kernel, n=n),
        grid=(n // tk, wd // tn),
        in_specs=[pl.BlockSpec((2 * tk, n), lambda i, j: (i, 0)), h_spec, h_spec],
        out_specs=pl.BlockSpec((2 * tk, tn), lambda i, j: (i, j)),
        out_shape=jax.ShapeDtypeStruct((2 * n, wd), F32),
        compiler_params=_params("parallel", "arbitrary"),
    )(dft, hs, hd)


def _dftconv_kernel(z_ref, m_ref, k_ref, kn_ref, o_ref, acc_ref, zn_ref, *, n):
    k = pl.program_id(2)
    tk = m_ref.shape[0] // 2
    zt = z_ref[...]
    row = lax.broadcasted_iota(jnp.int32, (n, 1), 0)
    sign = (1 - 2 * (row % 2)).astype(F32)

    @pl.when(k == 0)
    def _():
        zn_ref[...] = jnp.sum(zt.astype(F32) * sign, axis=0, keepdims=True)
        acc_ref[...] = jnp.zeros_like(acc_ref)

    m = m_ref[...]
    pq = _dot(m, zt)
    p, q = pq[0:tk], pq[tk:2 * tk]
    kr, ki = k_ref[0:tk, :], k_ref[tk:2 * tk, :]
    y = jnp.concatenate([p * kr + q * ki, q * kr - p * ki], axis=0).astype(BF)
    acc_ref[...] += lax.dot_general(m, y, _TN, preferred_element_type=F32)

    @pl.when(k == pl.num_programs(2) - 1)
    def _():
        nyq = zn_ref[...] * kn_ref[...] * (0.5 / n)
        o_ref[...] = (acc_ref[...] + sign * nyq).astype(o_ref.dtype)


def _dftconv_call(z, dft, kf, kn, batch):
    t, wd = z.shape
    n = t // batch
    tn = _tile(wd, 512, LANES)
    tk = _dft_tile(n)
    z_spec = pl.BlockSpec((n, tn), lambda b, c, k: (b, c))
    return pl.pallas_call(
        functools.partial(_dftconv_kernel, n=n),
        grid=(batch, wd // tn, n // tk),
        in_specs=[z_spec, pl.BlockSpec((2 * tk, n), lambda b, c, k: (k, 0)),
                  pl.BlockSpec((2 * tk, tn), lambda b, c, k: (k, c)),
                  pl.BlockSpec((1, tn), lambda b, c, k: (0, c))],
        out_specs=z_spec,
        out_shape=jax.ShapeDtypeStruct((t, wd), BF),
        scratch_shapes=[pltpu.VMEM((n, tn), F32), pltpu.VMEM((1, tn), F32)],
        compiler_params=_params("parallel", "parallel", "arbitrary"),
    )(z, dft, kf, kn)


def _mix_kernel(y_ref, x0_ref, z_ref, hbias_ref, of_ref, ob_ref, gz_ref, gn_ref, ghy_ref, ggdn_ref,
                why_ref, wgdn_ref, m_ref, uhy_ref, ugdn_ref, *, dv):
    @pl.when(pl.program_id(1) == 0)
    def _():
        z = z_ref[...].astype(F32)
        uhy_ref[...] = (x0_ref[...].astype(F32) * (y_ref[...].astype(F32) + z * hbias_ref[...])).astype(BF)
        gn = gn_ref[...]
        for s in range(of_ref.shape[1] // dv):
            sl = slice(s * dv, (s + 1) * dv)
            o = of_ref[:, sl] + ob_ref[:, sl]
            on = o * lax.rsqrt(jnp.mean(o * o, axis=-1, keepdims=True) + EPS) * gn
            ugdn_ref[:, sl] = (on * gz_ref[:, sl].astype(F32)).astype(BF)

    yh = _dot(uhy_ref[...], why_ref[...])
    yg = _dot(ugdn_ref[...], wgdn_ref[...])
    m_ref[...] = (ghy_ref[...].astype(F32) * yh + ggdn_ref[...].astype(F32) * yg).astype(m_ref.dtype)


def _mix_call(y, x0, z, hbias, o_f, o_b, gz, gn, gates, w_hy, w_gdn, dv):
    t, wd = y.shape
    hd = o_f.shape[1]
    d = w_hy.shape[1]
    tm = _tile(t, 256, 16)
    tn = _tile(d, 1024, LANES)
    nb = d // tn
    row = lambda c: pl.BlockSpec((tm, c), lambda i, j: (i, 0))
    col = lambda off: pl.BlockSpec((tm, tn), lambda i, j: (i, off + j))
    return pl.pallas_call(
        functools.partial(_mix_kernel, dv=dv),
        grid=(t // tm, nb),
        in_specs=[row(wd), row(wd), row(wd), pl.BlockSpec((1, wd), lambda i, j: (0, 0)),
                  row(hd), row(hd), row(hd), pl.BlockSpec((1, dv), lambda i, j: (0, 0)),
                  col(0), col(nb),
                  pl.BlockSpec((wd, tn), lambda i, j: (0, j)), pl.BlockSpec((hd, tn), lambda i, j: (0, j))],
        out_specs=pl.BlockSpec((tm, tn), lambda i, j: (i, j)),
        out_shape=jax.ShapeDtypeStruct((t, d), BF),
        scratch_shapes=[pltpu.VMEM((tm, wd), BF), pltpu.VMEM((tm, hd), BF)],
        compiler_params=_params("parallel", "arbitrary"),
    )(y, x0, z, hbias, o_f, o_b, gz, gn, gates, gates, w_hy, w_gdn)


def _resid_kernel(m_ref, w_ref, x_ref, ga_ref, o_ref):
    o_ref[...] = x_ref[...] + ga_ref[0] * _dot(m_ref[...], w_ref[...])


def _resid_call(m, w, x2, mod3, sec_ga, rows_per_mod):
    t, d = m.shape
    n = w.shape[1]
    tm = _tile(rows_per_mod, 512, 16)
    tn = _tile(n, 1024, LANES)
    per = rows_per_mod // tm
    nsec = n // tn
    return pl.pallas_call(
        _resid_kernel,
        grid=(t // tm, nsec),
        in_specs=[pl.BlockSpec((tm, d), lambda i, j: (i, 0)),
                  pl.BlockSpec((d, tn), lambda i, j: (0, j)),
                  pl.BlockSpec((tm, tn), lambda i, j: (i, j)),
                  pl.BlockSpec((1, 1, tn), lambda i, j: (i // per, 0, sec_ga * nsec + j))],
        out_specs=pl.BlockSpec((tm, tn), lambda i, j: (i, j)),
        out_shape=jax.ShapeDtypeStruct((t, n), F32),
        compiler_params=_params("parallel", "arbitrary"),
    )(m, w, x2, mod3)


def _ffn_up_kernel(x_ref, g_ref, sc_ref, sh_ref, wg_ref, wu_ref, o_ref, h_ref):
    @pl.when(pl.program_id(1) == 0)
    def _():
        x = x_ref[...]
        xn = x * lax.rsqrt(jnp.mean(x * x, axis=-1, keepdims=True) + EPS)
        h_ref[...] = ((xn * g_ref[...]) * (1.0 + sc_ref[0]) + sh_ref[0]).astype(BF)

    h = h_ref[...]
    o_ref[...] = (_silu(_dot(h, wg_ref[...])) * _dot(h, wu_ref[...])).astype(o_ref.dtype)


def _ffn_up_call(x2, g, mod3, sec_sh, sec_sc, w_up, rows_per_mod):
    t, d = x2.shape
    ff = w_up.shape[1] // 2
    tm = _tile(rows_per_mod, 512, 16)
    tn = _tile(ff, 512, LANES)
    nb = ff // tn
    per = rows_per_mod // tm
    mod_map = lambda sec: (lambda i, j: (i // per, 0, sec))
    return pl.pallas_call(
        _ffn_up_kernel,
        grid=(t // tm, nb),
        in_specs=[pl.BlockSpec((tm, d), lambda i, j: (i, 0)),
                  pl.BlockSpec((1, d), lambda i, j: (0, 0)),
                  pl.BlockSpec((1, 1, d), mod_map(sec_sc)),
                  pl.BlockSpec((1, 1, d), mod_map(sec_sh)),
                  pl.BlockSpec((d, tn), lambda i, j: (0, j)),
                  pl.BlockSpec((d, tn), lambda i, j: (0, nb + j))],
        out_specs=pl.BlockSpec((tm, tn), lambda i, j: (i, j)),
        out_shape=jax.ShapeDtypeStruct((t, ff), BF),
        scratch_shapes=[pltpu.VMEM((tm, d), BF)],
        compiler_params=_params("parallel", "arbitrary"),
    )(x2, g, mod3, mod3, w_up, w_up)


def _ffn_down_kernel(a_ref, w_ref, x_ref, ga_ref, gf_ref, o_ref, acc_ref):
    k = pl.program_id(1)

    @pl.when(k == 0)
    def _():
        acc_ref[...] = jnp.zeros_like(acc_ref)

    acc_ref[...] += _dot(a_ref[...], w_ref[...])

    @pl.when(k == pl.num_programs(1) - 1)
    def _():
        x = x_ref[...] + ga_ref[0] * acc_ref[...]
        o_ref[...] = x * lax.rsqrt(jnp.mean(x * x, axis=-1, keepdims=True) + EPS) * gf_ref[...]


def _ffn_down_call(a, w, x2, mod3, sec_ga, g_final, rows_per_mod):
    t, ff = a.shape
    d = w.shape[1]
    tm = _tile(rows_per_mod, 512, 16)
    tk = _tile(ff, 1408, LANES)
    per = rows_per_mod // tm
    return pl.pallas_call(
        _ffn_down_kernel,
        grid=(t // tm, ff // tk),
        in_specs=[pl.BlockSpec((tm, tk), lambda i, k: (i, k)),
                  pl.BlockSpec((tk, d), lambda i, k: (k, 0)),
                  pl.BlockSpec((tm, d), lambda i, k: (i, 0)),
                  pl.BlockSpec((1, 1, d), lambda i, k: (i // per, 0, sec_ga)),
                  pl.BlockSpec((1, d), lambda i, k: (0, 0))],
        out_specs=pl.BlockSpec((tm, d), lambda i, k: (i, 0)),
        out_shape=jax.ShapeDtypeStruct((t, d), F32),
        scratch_shapes=[pltpu.VMEM((tm, d), F32)],
        compiler_params=_params("parallel", "arbitrary"),
    )(a, w, x2, mod3, g_final)


def _group_columns(a, heads, hb):
    g = heads // hb
    lead = a.shape[:-1]
    a = a.reshape(*lead, 4, g, hb)
    a = jnp.moveaxis(a, -2, -3).reshape(*lead, g, 4 * hb)
    a = jnp.pad(a, [(0, 0)] * (len(lead) + 1) + [(0, LANES - 4 * hb)])
    return a.reshape(*lead, g * LANES)


def _layer(x, c, ctx, c_ctx, w_ada, b_ada, norm_mix, norm_ffn, w_in, hy_conv, hy_bias,
           hy_fw1, hy_fb1, hy_fw2, hy_fb2, hy_fw3, hy_fb3, hy_fout, hy_freq,
           gdn_conv, gdn_a_log, gdn_dt_bias, gdn_norm, w_hy_out, w_gdn_out, w_o, w_up, w_down,
           norm_final):
    b, n, d = x.shape
    n_ctx = ctx.shape[1]
    wd = hy_bias.shape[-1]
    heads = gdn_a_log.shape[-1]
    dv = gdn_norm.shape[-1]
    d_v = heads * dv
    d_qk = (gdn_conv.shape[-1] - d_v) // 2
    dk = d_qk // heads
    hb = min(4, heads)
    groups = heads // hb
    i_hy, i_qkv, i_z, i_sc = 3 * wd, 3 * wd + 2 * d_qk + d_v, 3 * wd + 2 * d_qk + 2 * d_v, 3 * wd + 2 * d_qk + 2 * d_v + 4 * heads

    rows = -(-(b + 1) // 8) * 8
    cvec = jnp.concatenate([jax.nn.silu(c), jax.nn.silu(c_ctx)[None], jnp.zeros((rows - b - 1, d), F32)], axis=0)
    mod = _ada_call(cvec.astype(BF), w_ada, b_ada.reshape(1, -1))
    mod3 = mod.reshape(rows, 1, 6 * d)

    x2 = x.reshape(b * n, d)
    ctx2 = ctx.reshape(b * n_ctx, d)
    g_mix = norm_mix.reshape(1, d)
    h_lat = _normmod_call(x2, g_mix, mod3, 0, 1, n, 0)
    h_ctx = _normmod_call(ctx2, g_mix, mod3, 0, 1, b * n_ctx, b)

    w_in16 = w_in.astype(BF)
    w_hy, w_qkv = w_in16[:, :i_hy], w_in16[:, i_hy:i_qkv]
    w_z, w_gate = w_in16[:, i_qkv:i_z], w_in16[:, i_sc:]
    w_scal = _group_columns(w_in[:, i_z:i_sc], heads, hb).astype(BF)
    a_vec = _group_columns(jnp.concatenate([jnp.zeros((2 * heads,), F32), gdn_a_log.reshape(-1)]), heads, hb)[None]
    dt_vec = _group_columns(jnp.concatenate([jnp.zeros((2 * heads,), F32), gdn_dt_bias.reshape(-1)]), heads, hb)[None]

    def gdn_branch(h, period, s0f, s0b):
        t = h.shape[0]
        qkv = _proj_qkv_call(h, w_qkv, gdn_conv, period, d_qk, dk)
        scal = _scal_call(h, w_scal, a_vec, dt_vec, hb)
        scal_t = scal.reshape(t // CHUNK, CHUNK, groups, LANES).transpose(0, 2, 3, 1)
        return _gdn_call(qkv, scal, scal_t, s0f, s0b, b, heads, dk, dv, hb)

    zero_state = jnp.zeros((b, heads, dk, dv), F32)
    _, _, s_f, s_b = gdn_branch(h_ctx, n_ctx, zero_state, zero_state)
    o_f, o_b, _, _ = gdn_branch(h_lat, GRID_W, s_f, s_b)

    x0, z = _proj_hy_call(h_lat, w_hy, hy_conv, GRID_W)
    tt = np.linspace(0.0, 1.0, n, dtype=np.float32)[:, None]
    bands = (hy_fw1.shape[0] - 1) // 2
    wv = (2.0 * math.pi * np.arange(n, dtype=np.float32)[:, None] / n).astype(np.float32)
    fv = np.linspace(1e-4, bands - 1, bands, dtype=np.float32)[None, :]
    zfeat = np.concatenate([tt, np.cos(fv * wv), -np.sin(fv * wv)], axis=-1).astype(np.float32)
    emb = zfeat.shape[1]
    emb_pad = -(-emb // 8) * 8
    zfeat = jnp.asarray(np.pad(zfeat, [(0, 0), (0, emb_pad - emb)]))
    fw1 = jnp.pad(hy_fw1, [(0, emb_pad - emb), (0, 0)])
    deltas = jnp.abs(jnp.linspace(HY_MIN_DECAY, HY_MAX_DECAY, wd, dtype=F32))[None]
    hs, hdiff, knyq = _filter_call(zfeat, fw1, hy_fb1[None], hy_fw2, hy_fb2[None], hy_fw3, hy_fb3[None],
                                   hy_freq[None], hy_fout, jnp.asarray(tt), deltas)
    dft = _dftmat_call(n)
    kf = _kf_call(dft, hs, hdiff)
    y = _dftconv_call(z, dft, kf, knyq, b)

    gz = _proj_act_call(h_lat, w_z, _silu)
    gates = _proj_act_call(h_lat, w_gate, _sigmoid)
    m = _mix_call(y, x0, z, hy_bias[None], o_f, o_b, gz, gdn_norm[None], gates,
                  w_hy_out.astype(BF), w_gdn_out.astype(BF), dv)
    x1 = _resid_call(m, w_o.astype(BF), x2, mod3, 2, n)

    a = _ffn_up_call(x1, norm_ffn.reshape(1, d), mod3, 3, 4, w_up.astype(BF), n)
    out = _ffn_down_call(a, w_down.astype(BF), x1, mod3, 5, norm_final.reshape(1, d), n)
    return out.reshape(b, n, d)


def kernel(x, c, ctx, c_ctx, w_ada, b_ada, norm_mix, norm_ffn, w_in, hy_conv, hy_bias, hy_fw1, hy_fb1, hy_fw2, hy_fb2, hy_fw3, hy_fb3, hy_fout, hy_freq, gdn_conv, gdn_a_log, gdn_dt_bias, gdn_norm, w_hy_out, w_gdn_out, w_o, w_up, w_down, norm_final):
    assert w_ada.shape[0] == 1, "single-layer block: the context stream is only read"
    return _layer(x, c, ctx, c_ctx, w_ada[0], b_ada[0], norm_mix[0], norm_ffn[0], w_in[0], hy_conv[0],
                  hy_bias[0], hy_fw1[0], hy_fb1[0], hy_fw2[0], hy_fb2[0], hy_fw3[0], hy_fb3[0],
                  hy_fout[0], hy_freq[0], gdn_conv[0], gdn_a_log[0], gdn_dt_bias[0], gdn_norm[0],
                  w_hy_out[0], w_gdn_out[0], w_o[0], w_up[0], w_down[0], norm_final)
```

```python
import functools
import math

import jax
import jax.numpy as jnp
import numpy as np
from jax import lax
from jax.experimental import pallas as pl
from jax.experimental.pallas import tpu as pltpu

BF = jnp.bfloat16
F32 = jnp.float32

GRID_W = 64
CHUNK = 64
SUB = 16
EPS = 1e-6
HY_MIN_DECAY = math.log(1e-2) / 1.5
HY_MAX_DECAY = math.log(1e-2) / 0.3
LANES = 128
VMEM_LIMIT = 56 * 1024 * 1024
ROW_TILE = 1024

_NT = (((1,), (1,)), ((), ()))
_TN = (((0,), (0,)), ((), ()))


def _tile(n, pref, mult):
    t = min(pref, n)
    t -= t % mult
    while t >= mult:
        if n % t == 0:
            return t
        t -= mult
    return n


def _params(*sem):
    return pltpu.CompilerParams(dimension_semantics=sem, vmem_limit_bytes=VMEM_LIMIT)


def _resident(shape):
    return pl.BlockSpec(shape, lambda *_: (0,) * len(shape), pipeline_mode=pl.Buffered(1))


def _dot(a, b):
    return jnp.dot(a, b, preferred_element_type=F32)


def _split3(x):
    hi = x.astype(BF)
    r = x - hi.astype(F32)
    mid = r.astype(BF)
    lo = (r - mid.astype(F32)).astype(BF)
    return hi, mid, lo


def _dot_f32(a, b):
    ah, am, _ = _split3(a)
    bh, bm, _ = _split3(b)
    return _dot(ah, bh) + (_dot(ah, bm) + _dot(am, bh))


def _silu(x):
    return x * (1.0 / (1.0 + jnp.exp(-x)))


def _sigmoid(x):
    return 1.0 / (1.0 + jnp.exp(-x))


def _ada_kernel(c_ref, w_ref, b_ref, o_ref):
    o_ref[...] = _dot(c_ref[...], w_ref[...].astype(BF)) + b_ref[...]


def _ada_call(cvec, w, b):
    d, n = w.shape
    r = cvec.shape[0]
    tn = _tile(n, 1024, LANES)
    return pl.pallas_call(
        _ada_kernel,
        name="ada",
        grid=(n // tn,),
        in_specs=[pl.BlockSpec((r, d), lambda j: (0, 0)),
                  pl.BlockSpec((d, tn), lambda j: (0, j)),
                  pl.BlockSpec((1, tn), lambda j: (0, j))],
        out_specs=pl.BlockSpec((r, tn), lambda j: (0, j)),
        out_shape=jax.ShapeDtypeStruct((r, n), F32),
        compiler_params=_params("parallel"),
    )(cvec, w, b)


def _normmod_kernel(x_ref, g_ref, sc_ref, sh_ref, o_ref):
    x = x_ref[...]
    xn = x * lax.rsqrt(jnp.mean(x * x, axis=-1, keepdims=True) + EPS)
    o_ref[...] = ((xn * g_ref[...]) * (1.0 + sc_ref[0]) + sh_ref[0]).astype(o_ref.dtype)


def _normmod_call(x2, g, mod3, sec_sh, sec_sc, rows_per_mod, mod_row0):
    t, d = x2.shape
    tm = _tile(rows_per_mod, 256, 8)
    per = rows_per_mod // tm
    mod_map = lambda sec: (lambda i: (mod_row0 + i // per, 0, sec))
    return pl.pallas_call(
        _normmod_kernel,
        name="normmod",
        grid=(t // tm,),
        in_specs=[pl.BlockSpec((tm, d), lambda i: (i, 0)),
                  pl.BlockSpec((1, d), lambda i: (0, 0)),
                  pl.BlockSpec((1, 1, d), mod_map(sec_sc)),
                  pl.BlockSpec((1, 1, d), mod_map(sec_sh))],
        out_specs=pl.BlockSpec((tm, d), lambda i: (i, 0)),
        out_shape=jax.ShapeDtypeStruct((t, d), BF),
        compiler_params=_params("parallel"),
    )(x2, g, mod3, mod3)


def _conv3(p, cw, period):
    tm = p.shape[0]
    row = lax.broadcasted_iota(jnp.int32, (tm, 1), 0) % period
    prev = jnp.where(row == 0, 0.0, pltpu.roll(p, 1, 0))
    nxt = jnp.where(row == period - 1, 0.0, pltpu.roll(p, tm - 1, 0))
    return prev * cw[0:1, :] + p * cw[1:2, :] + nxt * cw[2:3, :]


MXU_COLS = 256


def _col_blocks(n):
    sub = MXU_COLS if n % MXU_COLS == 0 else n
    return [slice(s, s + sub) for s in range(0, n, sub)]


def _proj_hy_kernel(h_ref, w0_ref, w1_ref, w2_ref, c0_ref, c1_ref, c2_ref, x0_ref, z_ref, *, period):
    h = h_ref[...]
    for sl in _col_blocks(x0_ref.shape[1]):
        x0 = _conv3(_dot(h, w0_ref[:, sl]), c0_ref[:, sl], period)
        x1 = _conv3(_dot(h, w1_ref[:, sl]), c1_ref[:, sl], period)
        v = _conv3(_dot(h, w2_ref[:, sl]), c2_ref[:, sl], period)
        x0_ref[:, sl] = x0.astype(x0_ref.dtype)
        z_ref[:, sl] = (x1 * v).astype(z_ref.dtype)


def _proj_hy_call(h, w, cw, period):
    t, d = h.shape
    wd = w.shape[1] // 3
    tm = _tile(t, ROW_TILE, period)
    tn = _tile(wd, 512, LANES)
    nb = wd // tn
    wspec = lambda s: pl.BlockSpec((d, tn), lambda i, j: (0, s * nb + j))
    cspec = lambda s: pl.BlockSpec((3, tn), lambda i, j: (0, s * nb + j))
    ospec = pl.BlockSpec((tm, tn), lambda i, j: (i, j))
    return pl.pallas_call(
        functools.partial(_proj_hy_kernel, period=period),
        name="proj_hy",
        grid=(t // tm, nb),
        in_specs=[pl.BlockSpec((tm, d), lambda i, j: (i, 0)), wspec(0), wspec(1), wspec(2),
                  cspec(0), cspec(1), cspec(2)],
        out_specs=[ospec, ospec],
        out_shape=[jax.ShapeDtypeStruct((t, wd), BF)] * 2,
        compiler_params=_params("parallel", "arbitrary"),
    )(h, w, w, w, cw, cw, cw)


def _proj_qkv_kernel(h_ref, w_ref, c_ref, o_ref, *, period, norm_dim):
    h = h_ref[...]
    for sl in _col_blocks(o_ref.shape[1]):
        u = _silu(_conv3(_dot(h, w_ref[:, sl]), c_ref[:, sl], period))
        if norm_dim:
            parts = []
            for s in range(0, u.shape[1], norm_dim):
                blk = u[:, s:s + norm_dim]
                parts.append(blk * lax.rsqrt(jnp.sum(blk * blk, axis=-1, keepdims=True) + EPS))
            u = jnp.concatenate(parts, axis=1)
        o_ref[:, sl] = u.astype(o_ref.dtype)


def _proj_qkv_call(h, w, cw, period, norm_dim):
    t, d = h.shape
    n = w.shape[1]
    tm = _tile(t, ROW_TILE, period)
    tn = _tile(n, 512, LANES)
    return pl.pallas_call(
        functools.partial(_proj_qkv_kernel, period=period, norm_dim=norm_dim),
        name="proj_qk" if norm_dim else "proj_v",
        grid=(t // tm, n // tn),
        in_specs=[pl.BlockSpec((tm, d), lambda i, j: (i, 0)),
                  pl.BlockSpec((d, tn), lambda i, j: (0, j)),
                  pl.BlockSpec((3, tn), lambda i, j: (0, j))],
        out_specs=pl.BlockSpec((tm, tn), lambda i, j: (i, j)),
        out_shape=jax.ShapeDtypeStruct((t, n), BF),
        compiler_params=_params("parallel", "arbitrary"),
    )(h, w, cw)


def _proj_act_kernel(h_ref, w_ref, o_ref, *, act):
    o_ref[...] = act(_dot(h_ref[...], w_ref[...])).astype(o_ref.dtype)


def _proj_act_call(h, w, act):
    t, d = h.shape
    n = w.shape[1]
    tm = _tile(t, ROW_TILE, 16)
    tn = _tile(n, 1024, LANES)
    return pl.pallas_call(
        functools.partial(_proj_act_kernel, act=act),
        name="proj_act",
        grid=(t // tm, n // tn),
        in_specs=[pl.BlockSpec((tm, d), lambda i, j: (i, 0)),
                  pl.BlockSpec((d, tn), lambda i, j: (0, j))],
        out_specs=pl.BlockSpec((tm, tn), lambda i, j: (i, j)),
        out_shape=jax.ShapeDtypeStruct((t, n), BF),
        compiler_params=_params("parallel", "arbitrary"),
    )(h, w)


def _scal_kernel(h_ref, w_ref, a_ref, dt_ref, o_ref, *, hb):
    s = _dot(h_ref[...], w_ref[...])
    tm = s.shape[0]
    lane = lax.broadcasted_iota(jnp.int32, (1, s.shape[1]), 1) % LANES
    beta = _sigmoid(s)
    xs = s + dt_ref[...]
    softplus = jnp.maximum(xs, 0.0) + jnp.log(1.0 + jnp.exp(-jnp.abs(xs)))
    g = jnp.where((lane >= 2 * hb) & (lane < 4 * hb), -jnp.exp(a_ref[...]) * softplus, 0.0)
    ri = lax.broadcasted_iota(jnp.int32, (tm, tm), 0)
    ci = lax.broadcasted_iota(jnp.int32, (tm, tm), 1)
    same = (ri // CHUNK) == (ci // CHUNK)
    lo = jnp.where(same & (ci <= ri), 1.0, 0.0).astype(BF)
    up = jnp.where(same & (ci >= ri), 1.0, 0.0).astype(BF)
    g0, g1, g2 = _split3(g)
    pre = _dot(lo, g0) + (_dot(lo, g1) + _dot(lo, g2))
    suf = _dot(up, g0) + (_dot(up, g1) + _dot(up, g2))
    o_ref[...] = jnp.where(lane < 2 * hb, beta, jnp.where(lane < 3 * hb, pre, jnp.where(lane < 4 * hb, suf, 0.0)))


def _scal_call(h, w, a_vec, dt_vec, hb):
    t, d = h.shape
    n = w.shape[1]
    tm = _tile(t, 256, CHUNK)
    return pl.pallas_call(
        functools.partial(_scal_kernel, hb=hb),
        name="scal",
        grid=(t // tm,),
        in_specs=[pl.BlockSpec((tm, d), lambda i: (i, 0)),
                  pl.BlockSpec((d, n), lambda i: (0, 0)),
                  pl.BlockSpec((1, n), lambda i: (0, 0)),
                  pl.BlockSpec((1, n), lambda i: (0, 0))],
        out_specs=pl.BlockSpec((tm, n), lambda i: (i, 0)),
        out_shape=jax.ShapeDtypeStruct((t, n), F32),
        compiler_params=_params("parallel"),
    )(h, w, a_vec, dt_vec)


def _tri_inverse_all(a_list):
    c = a_list[0].shape[0]
    ri = lax.broadcasted_iota(jnp.int32, (c, c), 0)
    ci = lax.broadcasted_iota(jnp.int32, (c, c), 1)
    same = (ri // SUB) == (ci // SUB)
    rs = lax.broadcasted_iota(jnp.int32, (SUB, c), 0)
    cs = lax.broadcasted_iota(jnp.int32, (SUB, c), 1)
    eye = jnp.where(rs == cs % SUB, 1.0, 0.0)

    def block_diag(xc):
        return jnp.where(same, jnp.concatenate([xc] * (c // SUB), axis=0), 0.0)

    xs = []
    for a in a_list:
        neg = jnp.where(same, -a, 0.0)
        xs.append(sum(neg[s:s + SUB] for s in range(0, c, SUB)))
    ps = [eye + x for x in xs]
    for _ in range(int(math.log2(SUB)) - 1):
        xbd = [block_diag(x).astype(BF) for x in xs]
        xs = [_dot(x.astype(BF), w) for x, w in zip(xs, xbd)]
        xbd = [block_diag(x).astype(BF) for x in xs]
        incs = [_dot(p.astype(BF), w) for p, w in zip(ps, xbd)]
        ps = [p + i for p, i in zip(ps, incs)]
    ps = [block_diag(p) for p in ps]
    size = SUB
    while size < c:
        sel = ((ri // (2 * size)) == (ci // (2 * size))) & ((ri // size) != (ci // size))
        p16 = [p.astype(BF) for p in ps]
        po = [_dot(p, jnp.where(sel, a, 0.0).astype(BF)).astype(BF) for p, a in zip(p16, a_list)]
        cor = [_dot(x, p) for x, p in zip(po, p16)]
        ps = [p - x for p, x in zip(ps, cor)]
        size *= 2
    return ps


def _gdn_chunk_all(chains, scale):
    c = chains[0][0].shape[0]
    dv = chains[0][2].shape[1]
    ri = lax.broadcasted_iota(jnp.int32, (c, c), 0)
    ci = lax.broadcasted_iota(jnp.int32, (c, c), 1)
    pre = []
    for q, k, v, sc, sct, s, lower, col_beta, col_g in chains:
        beta = sc[:, col_beta:col_beta + 1]
        gcol = sc[:, col_g:col_g + 1]
        grow = sct[col_g:col_g + 1, :]
        incl = (ri >= ci) if lower else (ri <= ci)
        strict = (ri > ci) if lower else (ri < ci)
        dec = jnp.exp(jnp.where(incl, gcol - grow, -jnp.inf))
        glast = gcol[c - 1:c, :] if lower else gcol[0:1, :]
        eg = jnp.exp(gcol)
        kf = k.astype(F32)
        kb = kf * beta
        rhs = jnp.concatenate([v.astype(F32) * beta, kb * eg], axis=1).astype(BF)
        qdec = (q.astype(F32) * (eg * scale)).astype(BF)
        kdec = (kf * jnp.exp(glast - gcol)).astype(BF)
        pre.append((incl, strict, dec, kb.astype(BF), rhs, qdec, kdec, jnp.exp(glast), s.astype(BF)))
    kk = [lax.dot_general(p[3], ch[1], _NT, preferred_element_type=F32) for p, ch in zip(pre, chains)]
    qk = [lax.dot_general(ch[0], ch[1], _NT, preferred_element_type=F32) for ch in chains]
    a = [jnp.where(p[1], x * p[2], 0.0) for p, x in zip(pre, kk)]
    qkm = [(jnp.where(p[0], x * p[2], 0.0) * scale).astype(BF) for p, x in zip(pre, qk)]
    t = _tri_inverse_all(a)
    sol = [_dot(x.astype(BF), p[4]) for x, p in zip(t, pre)]
    ws = [_dot(x[:, dv:].astype(BF), p[8]) for x, p in zip(sol, pre)]
    vn = [(x[:, :dv] - y).astype(BF) for x, y in zip(sol, ws)]
    o1 = [_dot(p[5], p[8]) for p in pre]
    o2 = [_dot(x, y) for x, y in zip(qkm, vn)]
    ds = [lax.dot_general(p[6], y, _TN, preferred_element_type=F32) for p, y in zip(pre, vn)]
    return [(x + y, ch[5] * p[7] + z) for x, y, z, p, ch in zip(o1, o2, ds, pre, chains)]


def _gdn_kernel(qf_ref, kf_ref, vf_ref, qb_ref, kb_ref, vb_ref, scf_ref, scb_ref, stf_ref, stb_ref,
                s0f_ref, s0b_ref, of_ref, ob_ref, sf_ref, sb_ref, st_ref, *, hb, dk, dv, scale):
    n = pl.program_id(2)

    @pl.when(n == 0)
    def _():
        st_ref[0] = s0f_ref[0]
        st_ref[1] = s0b_ref[0]

    scf, scb = scf_ref[...], scb_ref[...]
    stf, stb = stf_ref[0, 0], stb_ref[0, 0]
    chains = []
    for j in range(hb):
        chains.append((qf_ref[:, j * dk:(j + 1) * dk], kf_ref[:, j * dk:(j + 1) * dk],
                       vf_ref[:, j * dv:(j + 1) * dv], scf, stf, st_ref[0, j], True, j, 2 * hb + j))
        chains.append((qb_ref[:, j * dk:(j + 1) * dk], kb_ref[:, j * dk:(j + 1) * dk],
                       vb_ref[:, j * dv:(j + 1) * dv], scb, stb, st_ref[1, j], False, hb + j, 3 * hb + j))
    res = _gdn_chunk_all(chains, scale)
    for j in range(hb):
        o, s_new = res[2 * j]
        of_ref[:, j * dv:(j + 1) * dv] = o.astype(of_ref.dtype)
        st_ref[0, j] = s_new
        o, s_new = res[2 * j + 1]
        ob_ref[:, j * dv:(j + 1) * dv] = o.astype(ob_ref.dtype)
        st_ref[1, j] = s_new

    @pl.when(n == pl.num_programs(2) - 1)
    def _():
        sf_ref[0] = st_ref[0]
        sb_ref[0] = st_ref[1]


def _gdn_call(qk, v, scal, scal_t, s0f, s0b, batch, heads, dk, dv, hb):
    t = qk.shape[0]
    nc = t // batch // CHUNK
    g = heads // hb
    qb, vb = hb * dk, hb * dv
    fwd = lambda off: (lambda b, h, n: (b * nc + n, off + h))
    bwd = lambda off: (lambda b, h, n: (b * nc + nc - 1 - n, off + h))
    k_off = heads * dk // qb
    st_spec = pl.BlockSpec((1, hb, dk, dv), lambda b, h, n: (b, h, 0, 0))
    in_specs = [pl.BlockSpec((CHUNK, qb), fwd(0)), pl.BlockSpec((CHUNK, qb), fwd(k_off)),
                pl.BlockSpec((CHUNK, vb), fwd(0)),
                pl.BlockSpec((CHUNK, qb), bwd(0)), pl.BlockSpec((CHUNK, qb), bwd(k_off)),
                pl.BlockSpec((CHUNK, vb), bwd(0)),
                pl.BlockSpec((CHUNK, LANES), fwd(0)), pl.BlockSpec((CHUNK, LANES), bwd(0)),
                pl.BlockSpec((1, 1, LANES, CHUNK), lambda b, h, n: (b * nc + n, h, 0, 0)),
                pl.BlockSpec((1, 1, LANES, CHUNK), lambda b, h, n: (b * nc + nc - 1 - n, h, 0, 0)),
                st_spec, st_spec]
    out_specs = [pl.BlockSpec((CHUNK, vb), fwd(0)), pl.BlockSpec((CHUNK, vb), bwd(0)), st_spec, st_spec]
    o_sds = jax.ShapeDtypeStruct((t, heads * dv), BF)
    s_sds = jax.ShapeDtypeStruct((batch, heads, dk, dv), F32)
    return pl.pallas_call(
        functools.partial(_gdn_kernel, hb=hb, dk=dk, dv=dv, scale=dk ** -0.5),
        name="gdn",
        grid=(batch, g, nc),
        in_specs=in_specs,
        out_specs=out_specs,
        out_shape=[o_sds, o_sds, s_sds, s_sds],
        scratch_shapes=[pltpu.VMEM((2, hb, dk, dv), F32)],
        compiler_params=_params("parallel", "parallel", "arbitrary"),
    )(qk, qk, v, qk, qk, v, scal, scal, scal_t, scal_t, s0f, s0b)


def _filter_kernel(z_ref, w1_ref, b1_ref, w2_ref, b2_ref, w3_ref, b3_ref, fr_ref, ff_ref, fb_ref,
                   t_ref, dl_ref, hs_ref, hd_ref, kn_ref, h3_ref):
    @pl.when(pl.program_id(0) == 0)
    def _():
        fr = fr_ref[...]
        h = jnp.sin(fr * (_dot_f32(z_ref[...], w1_ref[...]) + b1_ref[...]))
        h = jnp.sin(fr * (_dot_f32(h, w2_ref[...]) + b2_ref[...]))
        h3_ref[...] = jnp.sin(fr * (_dot_f32(h, w3_ref[...]) + b3_ref[...]))

    h3 = h3_ref[...]
    n = h3.shape[0]
    dec = jnp.exp(-t_ref[...] * dl_ref[...])
    hf = _dot_f32(h3, ff_ref[...]) * dec
    row = lax.broadcasted_iota(jnp.int32, (n, 1), 0)
    hb = jnp.where(row == 0, 0.0, _dot_f32(h3, fb_ref[...]) * dec)
    hs = hf + hb
    hs_ref[...] = hs.astype(hs_ref.dtype)
    hd_ref[...] = (hb - hf).astype(hd_ref.dtype)
    sign = (1 - 2 * (row % 2)).astype(F32)
    kn_ref[...] = jnp.sum(hs * sign, axis=0, keepdims=True)


def _filter_call(zfeat, w1, b1, w2, b2, w3, b3, freq, fout, tcol, deltas):
    n, e = zfeat.shape
    o = w1.shape[1]
    wd = fout.shape[1] // 2
    tn = _tile(wd, 512, LANES)
    nb = wd // tn
    full = lambda a: pl.BlockSpec(a.shape, lambda j: (0,) * a.ndim)
    return pl.pallas_call(
        _filter_kernel,
        name="hyena_filter",
        grid=(nb,),
        in_specs=[full(zfeat), full(w1), full(b1), full(w2), full(b2), full(w3), full(b3), full(freq),
                  pl.BlockSpec((o, tn), lambda j: (0, j)), pl.BlockSpec((o, tn), lambda j: (0, nb + j)),
                  full(tcol), pl.BlockSpec((1, tn), lambda j: (0, j))],
        out_specs=[pl.BlockSpec((n, tn), lambda j: (0, j)), pl.BlockSpec((n, tn), lambda j: (0, j)),
                   pl.BlockSpec((1, tn), lambda j: (0, j))],
        out_shape=[jax.ShapeDtypeStruct((n, wd), BF), jax.ShapeDtypeStruct((n, wd), BF),
                   jax.ShapeDtypeStruct((1, wd), F32)],
        scratch_shapes=[pltpu.VMEM((n, o), F32)],
        compiler_params=_params("arbitrary"),
    )(zfeat, w1, b1, w2, b2, w3, b3, freq, fout, fout, tcol, deltas)


def _dftmat_kernel(ca_ref, sa_ref, cb_ref, sb_ref, m_ref):
    tk = ca_ref.shape[0]
    cb, sb = cb_ref[...], sb_ref[...]
    for t1 in range(ca_ref.shape[1]):
        ca = ca_ref[:, t1:t1 + 1]
        sa = sa_ref[:, t1:t1 + 1]
        m_ref[0:tk, t1 * LANES:(t1 + 1) * LANES] = (ca * cb - sa * sb).astype(m_ref.dtype)
        m_ref[tk:2 * tk, t1 * LANES:(t1 + 1) * LANES] = (sa * cb + ca * sb).astype(m_ref.dtype)


def _dft_tile(n):
    return _tile(n, 256, 16)


def _dftmat_call(n):
    n1 = n // LANES
    k = jnp.arange(n, dtype=jnp.int32)[:, None]
    ang_a = ((k * (LANES * jnp.arange(n1, dtype=jnp.int32)[None, :])) % (2 * n)).astype(F32) * (math.pi / n)
    ang_b = ((k * jnp.arange(LANES, dtype=jnp.int32)[None, :]) % (2 * n)).astype(F32) * (math.pi / n)
    tk = _dft_tile(n)
    a_spec = pl.BlockSpec((tk, n1), lambda i: (i, 0))
    b_spec = pl.BlockSpec((tk, LANES), lambda i: (i, 0))
    return pl.pallas_call(
        _dftmat_kernel,
        name="dftmat",
        grid=(n // tk,),
        in_specs=[a_spec, a_spec, b_spec, b_spec],
        out_specs=pl.BlockSpec((2 * tk, n), lambda i: (i, 0)),
        out_shape=jax.ShapeDtypeStruct((2 * n, n), BF),
        compiler_params=_params("parallel"),
    )(jnp.cos(ang_a), jnp.sin(ang_a), jnp.cos(ang_b), jnp.sin(ang_b))


def _kf_kernel(m_ref, hs_ref, hd_ref, k_ref, *, n):
    tk = m_ref.shape[0] // 2
    k = pl.program_id(0) * tk + lax.broadcasted_iota(jnp.int32, (tk, 1), 0)
    wk = jnp.where(k == 0, 0.5 / n, 1.0 / n)
    k_ref[0:tk, :] = _dot(m_ref[0:tk, :], hs_ref[...]) * wk
    k_ref[tk:2 * tk, :] = _dot(m_ref[tk:2 * tk, :], hd_ref[...]) * wk


def _kf_call(dft, hs, hd):
    n = dft.shape[1]
    wd = hs.shape[1]
    tk = _dft_tile(n)
    tn = _tile(wd, 512, LANES)
    h_spec = pl.BlockSpec((n, tn), lambda i, j: (0, j))
    return pl.pallas_call(
        functools.partial(_kf_kernel, n=n),
        name="filter_spectrum",
        grid=(n // tk, wd // tn),
        in_specs=[pl.BlockSpec((2 * tk, n), lambda i, j: (i, 0)), h_spec, h_spec],
        out_specs=pl.BlockSpec((2 * tk, tn), lambda i, j: (i, j)),
        out_shape=jax.ShapeDtypeStruct((2 * n, wd), F32),
        compiler_params=_params("parallel", "arbitrary"),
    )(dft, hs, hd)


def _dftconv_kernel(z_ref, m_ref, k_ref, kn_ref, o_ref, acc_ref, zn_ref, *, n):
    k = pl.program_id(2)
    tk = m_ref.shape[0] // 2
    zt = z_ref[...]
    row = lax.broadcasted_iota(jnp.int32, (n, 1), 0)
    sign = (1 - 2 * (row % 2)).astype(F32)

    @pl.when(k == 0)
    def _():
        zn_ref[...] = jnp.sum(zt.astype(F32) * sign, axis=0, keepdims=True)
        acc_ref[...] = jnp.zeros_like(acc_ref)

    m = m_ref[...]
    cols = _col_blocks(zt.shape[1])
    pqs = [_dot(m, z_ref[:, sl]) for sl in cols]
    ys = []
    for sl, pq in zip(cols, pqs):
        p, q = pq[0:tk], pq[tk:2 * tk]
        kr, ki = k_ref[0:tk, sl], k_ref[tk:2 * tk, sl]
        ys.append(jnp.concatenate([p * kr + q * ki, q * kr - p * ki], axis=0).astype(BF))
    for sl, y in zip(cols, ys):
        acc_ref[:, sl] += lax.dot_general(m, y, _TN, preferred_element_type=F32)

    @pl.when(k == pl.num_programs(2) - 1)
    def _():
        nyq = zn_ref[...] * kn_ref[...] * (0.5 / n)
        o_ref[...] = (acc_ref[...] + sign * nyq).astype(o_ref.dtype)


def _dftconv_call(z, dft, kf, kn, batch):
    t, wd = z.shape
    n = t // batch
    tn = _tile(wd, 512, LANES)
    tk = _dft_tile(n)
    z_spec = pl.BlockSpec((n, tn), lambda b, c, k: (b, c))
    return pl.pallas_call(
        functools.partial(_dftconv_kernel, n=n),
        name="dftconv",
        grid=(batch, wd // tn, n // tk),
        in_specs=[z_spec, pl.BlockSpec((2 * tk, n), lambda b, c, k: (k, 0)),
                  pl.BlockSpec((2 * tk, tn), lambda b, c, k: (k, c)),
                  pl.BlockSpec((1, tn), lambda b, c, k: (0, c))],
        out_specs=z_spec,
        out_shape=jax.ShapeDtypeStruct((t, wd), BF),
        scratch_shapes=[pltpu.VMEM((n, tn), F32), pltpu.VMEM((1, tn), F32)],
        compiler_params=_params("parallel", "parallel", "arbitrary"),
    )(z, dft, kf, kn)


def _mix_kernel(y_ref, x0_ref, z_ref, hbias_ref, of_ref, ob_ref, gz_ref, gn_ref, ghy_ref, ggdn_ref,
                why_ref, wgdn_ref, m_ref, *, dv, row_blocks):
    hbias, gn = hbias_ref[...], gn_ref[...]
    rb = m_ref.shape[0] // row_blocks
    ops = []
    for r in range(row_blocks):
        rs = slice(r * rb, (r + 1) * rb)
        z = z_ref[rs, :].astype(F32)
        uhy = (x0_ref[rs, :].astype(F32) * (y_ref[rs, :].astype(F32) + z * hbias)).astype(BF)
        parts = []
        for s in range(0, of_ref.shape[1], dv):
            o = of_ref[rs, s:s + dv].astype(F32) + ob_ref[rs, s:s + dv].astype(F32)
            on = o * lax.rsqrt(jnp.mean(o * o, axis=-1, keepdims=True) + EPS) * gn
            parts.append((on * gz_ref[rs, s:s + dv].astype(F32)).astype(BF))
        ops.append((rs, uhy, jnp.concatenate(parts, axis=1)))
    for rs, uhy, ugdn in ops:
        yh = _dot(uhy, why_ref[...])
        yg = _dot(ugdn, wgdn_ref[...])
        m_ref[rs, :] = (ghy_ref[rs, :].astype(F32) * yh + ggdn_ref[rs, :].astype(F32) * yg).astype(m_ref.dtype)


def _mix_call(y, x0, z, hbias, o_f, o_b, gz, gn, gates, w_hy, w_gdn, dv):
    t, wd = y.shape
    hd = o_f.shape[1]
    d = w_hy.shape[1]
    tm = _tile(t, 256, 32)
    row = lambda c: pl.BlockSpec((tm, c), lambda i: (i, 0))
    return pl.pallas_call(
        functools.partial(_mix_kernel, dv=dv, row_blocks=2),
        name="mix",
        grid=(t // tm,),
        in_specs=[row(wd), row(wd), row(wd), pl.BlockSpec((1, wd), lambda i: (0, 0)),
                  row(hd), row(hd), row(hd), pl.BlockSpec((1, dv), lambda i: (0, 0)),
                  pl.BlockSpec((tm, d), lambda i: (i, 0)), pl.BlockSpec((tm, d), lambda i: (i, 1)),
                  _resident((wd, d)), _resident((hd, d))],
        out_specs=pl.BlockSpec((tm, d), lambda i: (i, 0)),
        out_shape=jax.ShapeDtypeStruct((t, d), BF),
        compiler_params=_params("parallel"),
    )(y, x0, z, hbias, o_f, o_b, gz, gn, gates, gates, w_hy, w_gdn)


def _resid_kernel(m_ref, w_ref, x_ref, ga_ref, o_ref):
    o_ref[...] = x_ref[...] + ga_ref[0] * _dot(m_ref[...], w_ref[...])


def _resid_call(m, w, x2, mod3, sec_ga, rows_per_mod):
    t, d = m.shape
    n = w.shape[1]
    tm = _tile(rows_per_mod, 512, 16)
    per = rows_per_mod // tm
    return pl.pallas_call(
        _resid_kernel,
        name="out_resid",
        grid=(t // tm,),
        in_specs=[pl.BlockSpec((tm, d), lambda i: (i, 0)),
                  _resident((d, n)),
                  pl.BlockSpec((tm, n), lambda i: (i, 0)),
                  pl.BlockSpec((1, 1, n), lambda i: (i // per, 0, sec_ga))],
        out_specs=pl.BlockSpec((tm, n), lambda i: (i, 0)),
        out_shape=jax.ShapeDtypeStruct((t, n), F32),
        compiler_params=_params("parallel"),
    )(m, w, x2, mod3)


def _ffn_up_kernel(x_ref, g_ref, sc_ref, sh_ref, wg_ref, wu_ref, o_ref, h_ref):
    @pl.when(pl.program_id(1) == 0)
    def _():
        x = x_ref[...]
        xn = x * lax.rsqrt(jnp.mean(x * x, axis=-1, keepdims=True) + EPS)
        h_ref[...] = ((xn * g_ref[...]) * (1.0 + sc_ref[0]) + sh_ref[0]).astype(BF)

    h = h_ref[...]
    for sl in _col_blocks(o_ref.shape[1]):
        o_ref[:, sl] = (_silu(_dot(h, wg_ref[:, sl])) * _dot(h, wu_ref[:, sl])).astype(o_ref.dtype)


def _ffn_up_call(x2, g, mod3, sec_sh, sec_sc, w_up, rows_per_mod):
    t, d = x2.shape
    ff = w_up.shape[1] // 2
    tm = _tile(rows_per_mod, ROW_TILE, 16)
    tn = _tile(ff, 512, LANES)
    nb = ff // tn
    per = rows_per_mod // tm
    mod_map = lambda sec: (lambda i, j: (i // per, 0, sec))
    return pl.pallas_call(
        _ffn_up_kernel,
        name="ffn_up",
        grid=(t // tm, nb),
        in_specs=[pl.BlockSpec((tm, d), lambda i, j: (i, 0)),
                  pl.BlockSpec((1, d), lambda i, j: (0, 0)),
                  pl.BlockSpec((1, 1, d), mod_map(sec_sc)),
                  pl.BlockSpec((1, 1, d), mod_map(sec_sh)),
                  pl.BlockSpec((d, tn), lambda i, j: (0, j)),
                  pl.BlockSpec((d, tn), lambda i, j: (0, nb + j))],
        out_specs=pl.BlockSpec((tm, tn), lambda i, j: (i, j)),
        out_shape=jax.ShapeDtypeStruct((t, ff), BF),
        scratch_shapes=[pltpu.VMEM((tm, d), BF)],
        compiler_params=_params("parallel", "arbitrary"),
    )(x2, g, mod3, mod3, w_up, w_up)


def _ffn_down_kernel(a_ref, w_ref, x_ref, ga_ref, gf_ref, o_ref):
    x = x_ref[...] + ga_ref[0] * _dot(a_ref[...], w_ref[...])
    o_ref[...] = x * lax.rsqrt(jnp.mean(x * x, axis=-1, keepdims=True) + EPS) * gf_ref[...]


def _ffn_down_call(a, w, x2, mod3, sec_ga, g_final, rows_per_mod):
    t, ff = a.shape
    d = w.shape[1]
    tm = _tile(rows_per_mod, 512, 16)
    per = rows_per_mod // tm
    return pl.pallas_call(
        _ffn_down_kernel,
        name="ffn_down",
        grid=(t // tm,),
        in_specs=[pl.BlockSpec((tm, ff), lambda i: (i, 0)),
                  _resident((ff, d)),
                  pl.BlockSpec((tm, d), lambda i: (i, 0)),
                  pl.BlockSpec((1, 1, d), lambda i: (i // per, 0, sec_ga)),
                  pl.BlockSpec((1, d), lambda i: (0, 0))],
        out_specs=pl.BlockSpec((tm, d), lambda i: (i, 0)),
        out_shape=jax.ShapeDtypeStruct((t, d), F32),
        compiler_params=_params("parallel"),
    )(a, w, x2, mod3, g_final)


def _group_columns(a, heads, hb):
    g = heads // hb
    lead = a.shape[:-1]
    a = a.reshape(*lead, 4, g, hb)
    a = jnp.moveaxis(a, -2, -3).reshape(*lead, g, 4 * hb)
    a = jnp.pad(a, [(0, 0)] * (len(lead) + 1) + [(0, LANES - 4 * hb)])
    return a.reshape(*lead, g * LANES)


def _layer(x, c, ctx, c_ctx, w_ada, b_ada, norm_mix, norm_ffn, w_in, hy_conv, hy_bias,
           hy_fw1, hy_fb1, hy_fw2, hy_fb2, hy_fw3, hy_fb3, hy_fout, hy_freq,
           gdn_conv, gdn_a_log, gdn_dt_bias, gdn_norm, w_hy_out, w_gdn_out, w_o, w_up, w_down,
           norm_final):
    b, n, d = x.shape
    n_ctx = ctx.shape[1]
    wd = hy_bias.shape[-1]
    heads = gdn_a_log.shape[-1]
    dv = gdn_norm.shape[-1]
    d_v = heads * dv
    d_qk = (gdn_conv.shape[-1] - d_v) // 2
    dk = d_qk // heads
    hb = min(16, heads)
    groups = heads // hb
    i_hy, i_qkv, i_z, i_sc = 3 * wd, 3 * wd + 2 * d_qk + d_v, 3 * wd + 2 * d_qk + 2 * d_v, 3 * wd + 2 * d_qk + 2 * d_v + 4 * heads

    rows = -(-(b + 1) // 8) * 8
    cvec = jnp.concatenate([jax.nn.silu(c), jax.nn.silu(c_ctx)[None], jnp.zeros((rows - b - 1, d), F32)], axis=0)
    mod = _ada_call(cvec.astype(BF), w_ada, b_ada.reshape(1, -1))
    mod3 = mod.reshape(rows, 1, 6 * d)

    x2 = x.reshape(b * n, d)
    ctx2 = ctx.reshape(b * n_ctx, d)
    g_mix = norm_mix.reshape(1, d)
    h_lat = _normmod_call(x2, g_mix, mod3, 0, 1, n, 0)
    h_ctx = _normmod_call(ctx2, g_mix, mod3, 0, 1, b * n_ctx, b)

    w_in16 = w_in.astype(BF)
    w_hy, w_qkv = w_in16[:, :i_hy], w_in16[:, i_hy:i_qkv]
    w_z, w_gate = w_in16[:, i_qkv:i_z], w_in16[:, i_sc:]
    w_scal = _group_columns(w_in[:, i_z:i_sc], heads, hb).astype(BF)
    a_vec = _group_columns(jnp.concatenate([jnp.zeros((2 * heads,), F32), gdn_a_log.reshape(-1)]), heads, hb)[None]
    dt_vec = _group_columns(jnp.concatenate([jnp.zeros((2 * heads,), F32), gdn_dt_bias.reshape(-1)]), heads, hb)[None]

    def gdn_branch(h, period, s0f, s0b):
        t = h.shape[0]
        qk = _proj_qkv_call(h, w_qkv[:, :2 * d_qk], gdn_conv[:, :2 * d_qk], period, dk)
        v = _proj_qkv_call(h, w_qkv[:, 2 * d_qk:], gdn_conv[:, 2 * d_qk:], period, 0)
        scal = _scal_call(h, w_scal, a_vec, dt_vec, hb)
        scal_t = scal.reshape(t // CHUNK, CHUNK, groups, LANES).transpose(0, 2, 3, 1)
        return _gdn_call(qk, v, scal, scal_t, s0f, s0b, b, heads, dk, dv, hb)

    zero_state = jnp.zeros((b, heads, dk, dv), F32)
    _, _, s_f, s_b = gdn_branch(h_ctx, n_ctx, zero_state, zero_state)
    o_f, o_b, _, _ = gdn_branch(h_lat, GRID_W, s_f, s_b)

    x0, z = _proj_hy_call(h_lat, w_hy, hy_conv, GRID_W)
    tt = np.linspace(0.0, 1.0, n, dtype=np.float32)[:, None]
    bands = (hy_fw1.shape[0] - 1) // 2
    wv = (2.0 * math.pi * np.arange(n, dtype=np.float32)[:, None] / n).astype(np.float32)
    fv = np.linspace(1e-4, bands - 1, bands, dtype=np.float32)[None, :]
    zfeat = np.concatenate([tt, np.cos(fv * wv), -np.sin(fv * wv)], axis=-1).astype(np.float32)
    emb = zfeat.shape[1]
    emb_pad = -(-emb // 8) * 8
    zfeat = jnp.asarray(np.pad(zfeat, [(0, 0), (0, emb_pad - emb)]))
    fw1 = jnp.pad(hy_fw1, [(0, emb_pad - emb), (0, 0)])
    deltas = jnp.abs(jnp.linspace(HY_MIN_DECAY, HY_MAX_DECAY, wd, dtype=F32))[None]
    hs, hdiff, knyq = _filter_call(zfeat, fw1, hy_fb1[None], hy_fw2, hy_fb2[None], hy_fw3, hy_fb3[None],
                                   hy_freq[None], hy_fout, jnp.asarray(tt), deltas)
    dft = _dftmat_call(n)
    kf = _kf_call(dft, hs, hdiff)
    y = _dftconv_call(z, dft, kf, knyq, b)

    gz = _proj_act_call(h_lat, w_z, _silu)
    gates = _proj_act_call(h_lat, w_gate, _sigmoid)
    m = _mix_call(y, x0, z, hy_bias[None], o_f, o_b, gz, gdn_norm[None], gates,
                  w_hy_out.astype(BF), w_gdn_out.astype(BF), dv)
    x1 = _resid_call(m, w_o.astype(BF), x2, mod3, 2, n)

    a = _ffn_up_call(x1, norm_ffn.reshape(1, d), mod3, 3, 4, w_up.astype(BF), n)
    out = _ffn_down_call(a, w_down.astype(BF), x1, mod3, 5, norm_final.reshape(1, d), n)
    return out.reshape(b, n, d)


def kernel(x, c, ctx, c_ctx, w_ada, b_ada, norm_mix, norm_ffn, w_in, hy_conv, hy_bias, hy_fw1, hy_fb1, hy_fw2, hy_fb2, hy_fw3, hy_fb3, hy_fout, hy_freq, gdn_conv, gdn_a_log, gdn_dt_bias, gdn_norm, w_hy_out, w_gdn_out, w_o, w_up, w_down, norm_final):
    assert w_ada.shape[0] == 1, "single-layer block: the context stream is only read"
    return _layer(x, c, ctx, c_ctx, w_ada[0], b_ada[0], norm_mix[0], norm_ffn[0], w_in[0], hy_conv[0],
                  hy_bias[0], hy_fw1[0], hy_fb1[0], hy_fw2[0], hy_fb2[0], hy_fw3[0], hy_fb3[0],
                  hy_fout[0], hy_freq[0], gdn_conv[0], gdn_a_log[0], gdn_dt_bias[0], gdn_norm[0],
                  w_hy_out[0], w_gdn_out[0], w_o[0], w_up[0], w_down[0], norm_final)
```

```python
import functools
import math

import jax
import jax.numpy as jnp
import numpy as np
from jax import lax
from jax.experimental import pallas as pl
from jax.experimental.pallas import tpu as pltpu

BF = jnp.bfloat16
F32 = jnp.float32

GRID_W = 64
CHUNK = 64
SUB = 16
GDN_CHUNKS_PER_STEP = 1
EPS = 1e-6
HY_MIN_DECAY = math.log(1e-2) / 1.5
HY_MAX_DECAY = math.log(1e-2) / 0.3
LANES = 128
VMEM_LIMIT = 56 * 1024 * 1024
ROW_TILE = 1024

_NT = (((1,), (1,)), ((), ()))
_TN = (((0,), (0,)), ((), ()))


def _tile(n, pref, mult):
    t = min(pref, n)
    t -= t % mult
    while t >= mult:
        if n % t == 0:
            return t
        t -= mult
    return n


def _params(*sem):
    return pltpu.CompilerParams(dimension_semantics=sem, vmem_limit_bytes=VMEM_LIMIT)


def _resident(shape):
    return pl.BlockSpec(shape, lambda *_: (0,) * len(shape), pipeline_mode=pl.Buffered(1))


def _dot(a, b):
    return jnp.dot(a, b, preferred_element_type=F32)


def _split3(x):
    hi = x.astype(BF)
    r = x - hi.astype(F32)
    mid = r.astype(BF)
    lo = (r - mid.astype(F32)).astype(BF)
    return hi, mid, lo


def _dot_f32(a, b):
    ah, am, _ = _split3(a)
    bh, bm, _ = _split3(b)
    return _dot(ah, bh) + (_dot(ah, bm) + _dot(am, bh))


def _silu(x):
    return x * (1.0 / (1.0 + jnp.exp(-x)))


def _sigmoid(x):
    return 1.0 / (1.0 + jnp.exp(-x))


def _ada_kernel(c_ref, w_ref, b_ref, o_ref):
    o_ref[...] = _dot(c_ref[...], w_ref[...].astype(BF)) + b_ref[...]


def _ada_call(cvec, w, b):
    d, n = w.shape
    r = cvec.shape[0]
    tn = _tile(n, 1024, LANES)
    return pl.pallas_call(
        _ada_kernel,
        name="ada",
        grid=(n // tn,),
        in_specs=[pl.BlockSpec((r, d), lambda j: (0, 0)),
                  pl.BlockSpec((d, tn), lambda j: (0, j)),
                  pl.BlockSpec((1, tn), lambda j: (0, j))],
        out_specs=pl.BlockSpec((r, tn), lambda j: (0, j)),
        out_shape=jax.ShapeDtypeStruct((r, n), F32),
        compiler_params=_params("parallel"),
    )(cvec, w, b)


def _normmod_kernel(x_ref, g_ref, sc_ref, sh_ref, o_ref):
    x = x_ref[...]
    xn = x * lax.rsqrt(jnp.mean(x * x, axis=-1, keepdims=True) + EPS)
    o_ref[...] = ((xn * g_ref[...]) * (1.0 + sc_ref[0]) + sh_ref[0]).astype(o_ref.dtype)


def _normmod_call(x2, g, mod3, sec_sh, sec_sc, rows_per_mod, mod_row0):
    t, d = x2.shape
    tm = _tile(rows_per_mod, 256, 8)
    per = rows_per_mod // tm
    mod_map = lambda sec: (lambda i: (mod_row0 + i // per, 0, sec))
    return pl.pallas_call(
        _normmod_kernel,
        name="normmod",
        grid=(t // tm,),
        in_specs=[pl.BlockSpec((tm, d), lambda i: (i, 0)),
                  pl.BlockSpec((1, d), lambda i: (0, 0)),
                  pl.BlockSpec((1, 1, d), mod_map(sec_sc)),
                  pl.BlockSpec((1, 1, d), mod_map(sec_sh))],
        out_specs=pl.BlockSpec((tm, d), lambda i: (i, 0)),
        out_shape=jax.ShapeDtypeStruct((t, d), BF),
        compiler_params=_params("parallel"),
    )(x2, g, mod3, mod3)


MXU_COLS = 256


ROW_SUB = 512


def _col_blocks(n):
    sub = MXU_COLS if n % MXU_COLS == 0 else n
    return [slice(s, s + sub) for s in range(0, n, sub)]


def _sub_blocks(rows, cols):
    rsub = ROW_SUB if rows % ROW_SUB == 0 else rows
    return [(slice(r, r + rsub), cs) for cs in _col_blocks(cols) for r in range(0, rows, rsub)]


STAGE_PAD = 8


def _stage_scratch(rows, cols, n):
    rs, cs = _sub_blocks(rows, cols)[0]
    return pltpu.VMEM((2, n, rs.stop - rs.start + 2 * STAGE_PAD, cs.stop - cs.start), F32)


def _stage(stage_ref, slot, j, value):
    rows = value.shape[0]
    guard = jnp.zeros((1, value.shape[1]), F32)
    stage_ref[slot, j, STAGE_PAD - 1:STAGE_PAD, :] = guard
    stage_ref[slot, j, STAGE_PAD:STAGE_PAD + rows, :] = value
    stage_ref[slot, j, STAGE_PAD + rows:STAGE_PAD + rows + 1, :] = guard


def _conv3_staged(stage_ref, slot, j, rows, cw, period):
    row = lax.broadcasted_iota(jnp.int32, (rows, 1), 0) % period
    p = stage_ref[slot, j, STAGE_PAD:STAGE_PAD + rows, :]
    prev = jnp.where(row == 0, 0.0, stage_ref[slot, j, STAGE_PAD - 1:STAGE_PAD - 1 + rows, :])
    nxt = jnp.where(row == period - 1, 0.0, stage_ref[slot, j, STAGE_PAD + 1:STAGE_PAD + 1 + rows, :])
    return prev * cw[0:1, :] + p * cw[1:2, :] + nxt * cw[2:3, :]


def _pipelined(blocks, matmul, epilogue):
    pending = None
    for b in blocks:
        acc = matmul(b)
        if pending is not None:
            epilogue(*pending)
        pending = (b, acc)
    epilogue(*pending)


def _staged(blocks, matmul, epilogue):
    pending = None
    for k, b in enumerate(blocks):
        matmul(b, k % 2)
        if pending is not None:
            epilogue(*pending)
        pending = (b, k % 2)
    epilogue(*pending)


def _proj_hy_kernel(h_ref, w0_ref, w1_ref, w2_ref, c0_ref, c1_ref, c2_ref, x0_ref, z_ref, stage_ref, *, period):
    blocks = _sub_blocks(*x0_ref.shape)
    rows = blocks[0][0].stop - blocks[0][0].start

    def matmul(b, slot):
        h = h_ref[b[0], :]
        for j, w_ref in enumerate((w0_ref, w1_ref, w2_ref)):
            _stage(stage_ref, slot, j, _dot(h, w_ref[:, b[1]]))

    def epilogue(b, slot):
        rs, sl = b
        x0, x1, v = (_conv3_staged(stage_ref, slot, j, rows, c_ref[:, sl], period)
                     for j, c_ref in enumerate((c0_ref, c1_ref, c2_ref)))
        x0_ref[rs, sl] = x0.astype(x0_ref.dtype)
        z_ref[rs, sl] = (x1 * v).astype(z_ref.dtype)

    _staged(blocks, matmul, epilogue)


def _proj_hy_call(h, w, w_off, wd, cw, period):
    t, d = h.shape
    tm = _tile(t, ROW_TILE, period)
    tn = _tile(math.gcd(wd, w_off) if w_off else wd, 512, LANES)
    nb = wd // tn
    off = w_off // tn
    wspec = lambda s: pl.BlockSpec((d, tn), lambda j, i: (0, off + s * nb + j))
    cspec = lambda s: pl.BlockSpec((3, tn), lambda j, i: (0, s * nb + j))
    ospec = pl.BlockSpec((tm, tn), lambda j, i: (i, j))
    return pl.pallas_call(
        functools.partial(_proj_hy_kernel, period=period),
        name="proj_hy",
        grid=(nb, t // tm),
        in_specs=[pl.BlockSpec((tm, d), lambda j, i: (i, 0)), wspec(0), wspec(1), wspec(2),
                  cspec(0), cspec(1), cspec(2)],
        out_specs=[ospec, ospec],
        out_shape=[jax.ShapeDtypeStruct((t, wd), BF)] * 2,
        scratch_shapes=[_stage_scratch(tm, tn, 3)],
        compiler_params=_params("parallel", "arbitrary"),
    )(h, w, w, w, cw, cw, cw)


def _proj_qkv_kernel(h_ref, w_ref, c_ref, o_ref, stage_ref, *, period, norm_dim):
    blocks = _sub_blocks(*o_ref.shape)
    rows = blocks[0][0].stop - blocks[0][0].start

    def matmul(b, slot):
        _stage(stage_ref, slot, 0, _dot(h_ref[b[0], :], w_ref[:, b[1]]))

    def epilogue(b, slot):
        rs, sl = b
        u = _silu(_conv3_staged(stage_ref, slot, 0, rows, c_ref[:, sl], period))
        if norm_dim:
            parts = []
            for s in range(0, u.shape[1], norm_dim):
                blk = u[:, s:s + norm_dim]
                parts.append(blk * lax.rsqrt(jnp.sum(blk * blk, axis=-1, keepdims=True) + EPS))
            u = jnp.concatenate(parts, axis=1)
        o_ref[rs, sl] = u.astype(o_ref.dtype)

    _staged(blocks, matmul, epilogue)


def _proj_qkv_call(h, w, w_off, cw, c_off, n, period, norm_dim):
    t, d = h.shape
    tm = _tile(t, ROW_TILE, period)
    tn = _tile(math.gcd(math.gcd(n, w_off), c_off) if (w_off or c_off) else n, 512, LANES)
    woff, coff = w_off // tn, c_off // tn
    return pl.pallas_call(
        functools.partial(_proj_qkv_kernel, period=period, norm_dim=norm_dim),
        name="proj_qk" if norm_dim else "proj_v",
        grid=(n // tn, t // tm),
        in_specs=[pl.BlockSpec((tm, d), lambda j, i: (i, 0)),
                  pl.BlockSpec((d, tn), lambda j, i: (0, woff + j)),
                  pl.BlockSpec((3, tn), lambda j, i: (0, coff + j))],
        out_specs=pl.BlockSpec((tm, tn), lambda j, i: (i, j)),
        out_shape=jax.ShapeDtypeStruct((t, n), BF),
        scratch_shapes=[_stage_scratch(tm, tn, 1)],
        compiler_params=_params("parallel", "arbitrary"),
    )(h, w, cw)


def _proj_act_kernel(h_ref, w_ref, o_ref, *, act):
    def epilogue(b, acc):
        o_ref[b[0], b[1]] = act(acc).astype(o_ref.dtype)

    _pipelined(_sub_blocks(*o_ref.shape), lambda b: _dot(h_ref[b[0], :], w_ref[:, b[1]]), epilogue)


def _proj_act_call(h, w, w_off, n, act):
    t, d = h.shape
    tm = _tile(t, ROW_TILE, 16)
    tn = _tile(math.gcd(n, w_off) if w_off else n, 1024, LANES)
    woff = w_off // tn
    return pl.pallas_call(
        functools.partial(_proj_act_kernel, act=act),
        name="proj_act",
        grid=(n // tn, t // tm),
        in_specs=[pl.BlockSpec((tm, d), lambda j, i: (i, 0)),
                  pl.BlockSpec((d, tn), lambda j, i: (0, woff + j))],
        out_specs=pl.BlockSpec((tm, tn), lambda j, i: (i, j)),
        out_shape=jax.ShapeDtypeStruct((t, n), BF),
        compiler_params=_params("parallel", "arbitrary"),
    )(h, w)


def _scal_kernel(h_ref, w_ref, a_ref, dt_ref, o_ref, *, hb):
    s = _dot(h_ref[...], w_ref[...])
    tm = s.shape[0]
    lane = lax.broadcasted_iota(jnp.int32, (1, s.shape[1]), 1) % LANES
    beta = _sigmoid(s)
    xs = s + dt_ref[...]
    softplus = jnp.maximum(xs, 0.0) + jnp.log(1.0 + jnp.exp(-jnp.abs(xs)))
    g = jnp.where((lane >= 2 * hb) & (lane < 4 * hb), -jnp.exp(a_ref[...]) * softplus, 0.0)
    ri = lax.broadcasted_iota(jnp.int32, (tm, tm), 0)
    ci = lax.broadcasted_iota(jnp.int32, (tm, tm), 1)
    same = (ri // CHUNK) == (ci // CHUNK)
    lo = jnp.where(same & (ci <= ri), 1.0, 0.0).astype(BF)
    up = jnp.where(same & (ci >= ri), 1.0, 0.0).astype(BF)
    g0, g1, g2 = _split3(g)
    pre = _dot(lo, g0) + (_dot(lo, g1) + _dot(lo, g2))
    suf = _dot(up, g0) + (_dot(up, g1) + _dot(up, g2))
    o_ref[...] = jnp.where(lane < 2 * hb, beta, jnp.where(lane < 3 * hb, pre, jnp.where(lane < 4 * hb, suf, 0.0)))


def _scal_call(h, w, a_vec, dt_vec, hb):
    t, d = h.shape
    n = w.shape[1]
    tm = _tile(t, 256, CHUNK)
    return pl.pallas_call(
        functools.partial(_scal_kernel, hb=hb),
        name="scal",
        grid=(t // tm,),
        in_specs=[pl.BlockSpec((tm, d), lambda i: (i, 0)),
                  pl.BlockSpec((d, n), lambda i: (0, 0)),
                  pl.BlockSpec((1, n), lambda i: (0, 0)),
                  pl.BlockSpec((1, n), lambda i: (0, 0))],
        out_specs=pl.BlockSpec((tm, n), lambda i: (i, 0)),
        out_shape=jax.ShapeDtypeStruct((t, n), F32),
        compiler_params=_params("parallel"),
    )(h, w, a_vec, dt_vec)


def _tri_inverse_all(a_list):
    c = a_list[0].shape[0]
    ri = lax.broadcasted_iota(jnp.int32, (c, c), 0)
    ci = lax.broadcasted_iota(jnp.int32, (c, c), 1)
    same = (ri // SUB) == (ci // SUB)
    rs = lax.broadcasted_iota(jnp.int32, (SUB, c), 0)
    cs = lax.broadcasted_iota(jnp.int32, (SUB, c), 1)
    eye = jnp.where(rs == cs % SUB, 1.0, 0.0)

    def block_diag(xc):
        return jnp.where(same, jnp.concatenate([xc] * (c // SUB), axis=0), 0.0)

    xs = []
    for a in a_list:
        neg = jnp.where(same, -a, 0.0)
        xs.append(sum(neg[s:s + SUB] for s in range(0, c, SUB)))
    ps = [eye + x for x in xs]
    for _ in range(int(math.log2(SUB)) - 1):
        xbd = [block_diag(x).astype(BF) for x in xs]
        xs = [_dot(x.astype(BF), w) for x, w in zip(xs, xbd)]
        xbd = [block_diag(x).astype(BF) for x in xs]
        incs = [_dot(p.astype(BF), w) for p, w in zip(ps, xbd)]
        ps = [p + i for p, i in zip(ps, incs)]
    ps = [block_diag(p) for p in ps]
    size = SUB
    while size < c:
        sel = ((ri // (2 * size)) == (ci // (2 * size))) & ((ri // size) != (ci // size))
        p16 = [p.astype(BF) for p in ps]
        po = [_dot(p, jnp.where(sel, a, 0.0).astype(BF)).astype(BF) for p, a in zip(p16, a_list)]
        cor = [_dot(x, p) for x, p in zip(po, p16)]
        ps = [p - x for p, x in zip(ps, cor)]
        size *= 2
    return ps


def _gdn_prepare(chains, scale):
    c = chains[0][0].shape[0]
    dv = chains[0][2].shape[1]
    ri = lax.broadcasted_iota(jnp.int32, (c, c), 0)
    ci = lax.broadcasted_iota(jnp.int32, (c, c), 1)
    pre = []
    for q, k, v, sc, sct, lower, col_beta, col_g in chains:
        beta = sc[:, col_beta:col_beta + 1]
        gcol = sc[:, col_g:col_g + 1]
        grow = sct[col_g:col_g + 1, :]
        incl = (ri >= ci) if lower else (ri <= ci)
        strict = (ri > ci) if lower else (ri < ci)
        dec = jnp.exp(jnp.where(incl, gcol - grow, -jnp.inf))
        glast = gcol[c - 1:c, :] if lower else gcol[0:1, :]
        eg = jnp.exp(gcol)
        kf = k.astype(F32)
        kb = kf * beta
        rhs = jnp.concatenate([v.astype(F32) * beta, kb * eg], axis=1).astype(BF)
        qdec = (q.astype(F32) * (eg * scale)).astype(BF)
        kdec = (kf * jnp.exp(glast - gcol)).astype(BF)
        pre.append((incl, strict, dec, kb.astype(BF), rhs, qdec, kdec, jnp.exp(glast)))
    kq = [lax.dot_general(jnp.concatenate([p[3], ch[0]], axis=0), ch[1], _NT, preferred_element_type=F32)
          for p, ch in zip(pre, chains)]
    a = [jnp.where(p[1], x[0:c] * p[2], 0.0) for p, x in zip(pre, kq)]
    qkm = [(jnp.where(p[0], x[c:2 * c] * p[2], 0.0) * scale).astype(BF) for p, x in zip(pre, kq)]
    t = _tri_inverse_all(a)
    sol = [_dot(x.astype(BF), p[4]) for x, p in zip(t, pre)]
    return [(x[:, :dv], x[:, dv:].astype(BF), m, p[5], p[6], p[7]) for x, m, p in zip(sol, qkm, pre)]


def _gdn_advance(prep, states):
    c = prep[0][0].shape[0]
    s16 = [s.astype(BF) for s in states]
    wq = [_dot(jnp.concatenate([p[1], p[3]], axis=0), s) for p, s in zip(prep, s16)]
    vn = [(p[0] - x[0:c]).astype(BF) for p, x in zip(prep, wq)]
    o1 = [x[c:2 * c] for x in wq]
    o2 = [_dot(p[2], y) for p, y in zip(prep, vn)]
    ds = [lax.dot_general(p[4], y, _TN, preferred_element_type=F32) for p, y in zip(prep, vn)]
    return [x + y for x, y in zip(o1, o2)], [s * p[5] + z for s, p, z in zip(states, prep, ds)]


def _gdn_kernel(qf_ref, kf_ref, vf_ref, qb_ref, kb_ref, vb_ref, scf_ref, scb_ref, stf_ref, stb_ref,
                s0f_ref, s0b_ref, of_ref, ob_ref, sf_ref, sb_ref, st_ref, *, hb, dk, dv, scale):
    n = pl.program_id(2)

    @pl.when(n == 0)
    def _():
        st_ref[0] = s0f_ref[0]
        st_ref[1] = s0b_ref[0]

    sub = qf_ref.shape[0] // CHUNK
    chains, rows = [], []
    for a in range(sub):
        rf = slice(a * CHUNK, (a + 1) * CHUNK)
        rb = slice((sub - 1 - a) * CHUNK, (sub - a) * CHUNK)
        rows.append((rf, rb))
        for j in range(hb):
            qs, vs = slice(j * dk, (j + 1) * dk), slice(j * dv, (j + 1) * dv)
            chains.append((qf_ref[rf, qs], kf_ref[rf, qs], vf_ref[rf, vs], scf_ref[rf, :], stf_ref[a, 0],
                           True, j, 2 * hb + j))
            chains.append((qb_ref[rb, qs], kb_ref[rb, qs], vb_ref[rb, vs], scb_ref[rb, :],
                           stb_ref[sub - 1 - a, 0], False, hb + j, 3 * hb + j))
    prep = _gdn_prepare(chains, scale)
    states = [st_ref[d, j] for j in range(hb) for d in range(2)]
    for a, (rf, rb) in enumerate(rows):
        outs, states = _gdn_advance(prep[a * 2 * hb:(a + 1) * 2 * hb], states)
        for j in range(hb):
            vs = slice(j * dv, (j + 1) * dv)
            of_ref[rf, vs] = outs[2 * j].astype(of_ref.dtype)
            ob_ref[rb, vs] = outs[2 * j + 1].astype(ob_ref.dtype)
    for j in range(hb):
        st_ref[0, j] = states[2 * j]
        st_ref[1, j] = states[2 * j + 1]

    @pl.when(n == pl.num_programs(2) - 1)
    def _():
        sf_ref[0] = st_ref[0]
        sb_ref[0] = st_ref[1]


def _gdn_call(qk, v, scal, scal_t, s0f, s0b, batch, heads, dk, dv, hb):
    t = qk.shape[0]
    sub = GDN_CHUNKS_PER_STEP if (t // batch // CHUNK) % GDN_CHUNKS_PER_STEP == 0 else 1
    rows = sub * CHUNK
    nc = t // batch // rows
    g = heads // hb
    qb, vb = hb * dk, hb * dv
    fwd = lambda off: (lambda b, h, n: (b * nc + n, off + h))
    bwd = lambda off: (lambda b, h, n: (b * nc + nc - 1 - n, off + h))
    k_off = heads * dk // qb
    st_spec = pl.BlockSpec((1, hb, dk, dv), lambda b, h, n: (b, h, 0, 0))
    in_specs = [pl.BlockSpec((rows, qb), fwd(0)), pl.BlockSpec((rows, qb), fwd(k_off)),
                pl.BlockSpec((rows, vb), fwd(0)),
                pl.BlockSpec((rows, qb), bwd(0)), pl.BlockSpec((rows, qb), bwd(k_off)),
                pl.BlockSpec((rows, vb), bwd(0)),
                pl.BlockSpec((rows, LANES), fwd(0)), pl.BlockSpec((rows, LANES), bwd(0)),
                pl.BlockSpec((sub, 1, LANES, CHUNK), lambda b, h, n: (b * nc + n, h, 0, 0)),
                pl.BlockSpec((sub, 1, LANES, CHUNK), lambda b, h, n: (b * nc + nc - 1 - n, h, 0, 0)),
                st_spec, st_spec]
    out_specs = [pl.BlockSpec((rows, vb), fwd(0)), pl.BlockSpec((rows, vb), bwd(0)), st_spec, st_spec]
    o_sds = jax.ShapeDtypeStruct((t, heads * dv), BF)
    s_sds = jax.ShapeDtypeStruct((batch, heads, dk, dv), F32)
    return pl.pallas_call(
        functools.partial(_gdn_kernel, hb=hb, dk=dk, dv=dv, scale=dk ** -0.5),
        name="gdn",
        grid=(batch, g, nc),
        in_specs=in_specs,
        out_specs=out_specs,
        out_shape=[o_sds, o_sds, s_sds, s_sds],
        scratch_shapes=[pltpu.VMEM((2, hb, dk, dv), F32)],
        compiler_params=_params("parallel", "parallel", "arbitrary"),
    )(qk, qk, v, qk, qk, v, scal, scal, scal_t, scal_t, s0f, s0b)


def _filter_kernel(z_ref, w1_ref, b1_ref, w2_ref, b2_ref, w3_ref, b3_ref, fr_ref, ff_ref, fb_ref,
                   t_ref, dl_ref, hs_ref, hd_ref, h3_ref):
    @pl.when(pl.program_id(0) == 0)
    def _():
        fr = fr_ref[...]
        h = jnp.sin(fr * (_dot_f32(z_ref[...], w1_ref[...]) + b1_ref[...]))
        h = jnp.sin(fr * (_dot_f32(h, w2_ref[...]) + b2_ref[...]))
        h3_ref[...] = jnp.sin(fr * (_dot_f32(h, w3_ref[...]) + b3_ref[...]))

    h3 = h3_ref[...]
    n = h3.shape[0]
    dec = jnp.exp(-t_ref[...] * dl_ref[...])
    hf = _dot_f32(h3, ff_ref[...]) * dec
    row = lax.broadcasted_iota(jnp.int32, (n, 1), 0)
    hb = jnp.where(row == 0, 0.0, _dot_f32(h3, fb_ref[...]) * dec)
    hs_ref[...] = (hf + hb).astype(hs_ref.dtype)
    hd_ref[...] = (hb - hf).astype(hd_ref.dtype)


def _filter_call(zfeat, w1, b1, w2, b2, w3, b3, freq, fout, tcol, deltas):
    n, e = zfeat.shape
    o = w1.shape[1]
    wd = fout.shape[1] // 2
    tn = _tile(wd, 512, LANES)
    nb = wd // tn
    full = lambda a: pl.BlockSpec(a.shape, lambda j: (0,) * a.ndim)
    return pl.pallas_call(
        _filter_kernel,
        name="hyena_filter",
        grid=(nb,),
        in_specs=[full(zfeat), full(w1), full(b1), full(w2), full(b2), full(w3), full(b3), full(freq),
                  pl.BlockSpec((o, tn), lambda j: (0, j)), pl.BlockSpec((o, tn), lambda j: (0, nb + j)),
                  full(tcol), pl.BlockSpec((1, tn), lambda j: (0, j))],
        out_specs=[pl.BlockSpec((n, tn), lambda j: (0, j)), pl.BlockSpec((n, tn), lambda j: (0, j))],
        out_shape=[jax.ShapeDtypeStruct((n, wd), BF), jax.ShapeDtypeStruct((n, wd), BF)],
        scratch_shapes=[pltpu.VMEM((n, o), F32)],
        compiler_params=_params("arbitrary"),
    )(zfeat, w1, b1, w2, b2, w3, b3, freq, fout, fout, tcol, deltas)


def _dftmat_kernel(ca_ref, sa_ref, cb_ref, sb_ref, m_ref):
    tk = ca_ref.shape[0]
    cb, sb = cb_ref[...], sb_ref[...]
    for t1 in range(ca_ref.shape[1]):
        ca = ca_ref[:, t1:t1 + 1]
        sa = sa_ref[:, t1:t1 + 1]
        m_ref[0:tk, t1 * LANES:(t1 + 1) * LANES] = (ca * cb - sa * sb).astype(m_ref.dtype)
        m_ref[tk:2 * tk, t1 * LANES:(t1 + 1) * LANES] = (sa * cb + ca * sb).astype(m_ref.dtype)


def _dftmat_call(n, rows, tk):
    n1 = n // LANES
    k = jnp.arange(rows, dtype=jnp.int32)[:, None]
    ang_a = ((k * (LANES * jnp.arange(n1, dtype=jnp.int32)[None, :])) % (2 * n)).astype(F32) * (math.pi / n)
    ang_b = ((k * jnp.arange(LANES, dtype=jnp.int32)[None, :]) % (2 * n)).astype(F32) * (math.pi / n)
    a_spec = pl.BlockSpec((tk, n1), lambda i: (i, 0))
    b_spec = pl.BlockSpec((tk, LANES), lambda i: (i, 0))
    return pl.pallas_call(
        _dftmat_kernel,
        name="dftmat",
        grid=(rows // tk,),
        in_specs=[a_spec, a_spec, b_spec, b_spec],
        out_specs=pl.BlockSpec((2 * tk, n), lambda i: (i, 0)),
        out_shape=jax.ShapeDtypeStruct((2 * rows, n), BF),
        compiler_params=_params("parallel"),
    )(jnp.cos(ang_a), jnp.sin(ang_a), jnp.cos(ang_b), jnp.sin(ang_b))


def _twiddles(half, l2):
    i = np.arange(half, dtype=np.float64)
    rows = []
    for k in (i, l2 - i, l2 + i, 2 * l2 - i):
        ang = np.pi * k / (2 * l2)
        rows += [np.cos(ang), np.sin(ang)]
    tw = np.stack(rows).astype(np.float32)[:, :, None]
    return jnp.asarray(np.broadcast_to(tw, (8, half, LANES)).copy())


def _wide(t, n):
    return t if n == LANES else jnp.concatenate([t] * (n // LANES), axis=1)


def _spectrum_rows(pq, tw, tn):
    tk = pq.shape[0] // 2
    p0, p0a, p1, p1a = (pq[0:tk, j * tn:(j + 1) * tn] for j in range(4))
    q0, q0a, q1, q1a = (pq[tk:2 * tk, j * tn:(j + 1) * tn] for j in range(4))
    c1, s1, c2, s2, c3, s3, c4, s4 = tw
    return [(p0 + c1 * p1 - s1 * q1, -q0 - c1 * q1 - s1 * p1),
            (p0a + c2 * p1a + s2 * q1a, q0a + c2 * q1a - s2 * p1a),
            (p0a + c3 * p1a - s3 * q1a, -q0a - c3 * q1a - s3 * p1a),
            (p0 + c4 * p1 + s4 * q1, q0 + c4 * q1 - s4 * p1)]


_RT2 = math.sqrt(0.5)


def _mid_spectrum(pm0, qm0, pm1, qm1):
    zh = (pm0 + _RT2 * pm1 - _RT2 * qm1, -qm0 - _RT2 * qm1 - _RT2 * pm1)
    z3 = (pm0 - _RT2 * pm1 + _RT2 * qm1, qm0 - _RT2 * qm1 - _RT2 * pm1)
    return zh, z3


def _quarter_patterns(rows):
    m = lax.broadcasted_iota(jnp.int32, (rows, 1), 0) % 4
    cm = jnp.where(m == 0, 1.0, jnp.where(m == 2, -1.0, 0.0))
    sm = jnp.where(m == 1, 1.0, jnp.where(m == 3, -1.0, 0.0))
    return cm, sm, (1 - 2 * (m % 2)).astype(F32)


def _stage_inputs(x_ref, mid_ref, stage_ref, src_ref, pair):
    n, tn = src_ref.shape
    cm, sm, sign = _quarter_patterns(n // 2)
    for s in range(tn // LANES):
        lanes = slice(s * LANES, (s + 1) * LANES)
        stage_ref[s] = src_ref[:, lanes].astype(F32)
        for r in range(2):
            xf = stage_ref[s, pl.ds(r, n // 2, stride=2), :]
            col = (4 * pair + 2 * r) * tn + s * LANES
            x_ref[:, col:col + LANES] = xf.astype(BF)
            x_ref[:, col + tn:col + tn + LANES] = (xf * sign).astype(BF)
            row = 4 * pair + 2 * r
            mid_ref[row:row + 1, lanes] = jnp.sum(xf * cm, axis=0, keepdims=True)
            mid_ref[row + 1:row + 2, lanes] = jnp.sum(xf * sm, axis=0, keepdims=True)


def _filter_fft_kernel(hs_ref, hd_ref, m_ref, tw_ref, k_ref, kmid_ref, x_ref, mid_ref, stage_ref, *, l2):
    it = pl.program_id(1)
    tk = m_ref.shape[0] // 2
    tn = hs_ref.shape[1]

    @pl.when(it == 0)
    def _():
        _stage_inputs(x_ref, mid_ref, stage_ref, hs_ref, 0)
        _stage_inputs(x_ref, mid_ref, stage_ref, hd_ref, 1)
        zh_s, z3_s = _mid_spectrum(mid_ref[0:1, :], mid_ref[1:2, :], mid_ref[2:3, :], mid_ref[3:4, :])
        zh_d, z3_d = _mid_spectrum(mid_ref[4:5, :], mid_ref[5:6, :], mid_ref[6:7, :], mid_ref[7:8, :])
        scale = 0.5 / l2
        kmid_ref[0:1, :] = zh_s[0] * scale
        kmid_ref[1:2, :] = -zh_d[1] * scale
        kmid_ref[2:3, :] = z3_s[0] * scale
        kmid_ref[3:4, :] = -z3_d[1] * scale

    pq = _dot(m_ref[...], x_ref[...])
    tw = [_wide(tw_ref[j], tn) for j in range(8)]
    zs = _spectrum_rows(pq[:, 0:4 * tn], tw, tn)
    zd = _spectrum_rows(pq[:, 4 * tn:8 * tn], tw, tn)
    i = it * tk + lax.broadcasted_iota(jnp.int32, (tk, 1), 0)
    scale = jnp.where(i == 0, 0.25 / l2, 0.5 / l2)
    for j in range(4):
        k_ref[2 * j] = zs[j][0] * scale
        k_ref[2 * j + 1] = -zd[j][1] * scale


def _filter_fft_call(hs, hd, dft, tw, tk):
    n, wd = hs.shape
    l2 = n // 2
    half = l2 // 2
    tn = _tile(wd, 256, LANES)
    h_spec = pl.BlockSpec((n, tn), lambda c, i: (0, c))
    return pl.pallas_call(
        functools.partial(_filter_fft_kernel, l2=l2),
        name="filter_fft",
        grid=(wd // tn, half // tk),
        in_specs=[h_spec, h_spec,
                  pl.BlockSpec((2 * tk, l2), lambda c, i: (i, 0)),
                  pl.BlockSpec((8, tk, LANES), lambda c, i: (0, i, 0))],
        out_specs=[pl.BlockSpec((8, tk, tn), lambda c, i: (0, i, c)),
                   pl.BlockSpec((4, tn), lambda c, i: (0, c))],
        out_shape=[jax.ShapeDtypeStruct((8, half, wd), F32), jax.ShapeDtypeStruct((4, wd), F32)],
        scratch_shapes=[pltpu.VMEM((l2, 8 * tn), BF), pltpu.VMEM((8, tn), F32),
                        pltpu.VMEM((tn // LANES, n, LANES), F32)],
        compiler_params=_params("parallel", "arbitrary"),
    )(hs, hd, dft, tw)


def _fftconv_kernel(z_ref, m_ref, tw_ref, k_ref, kmid_ref, y_ref, x_ref, acc_ref, mid_ref, stage_ref):
    it = pl.program_id(2)
    n, tn = z_ref.shape
    l2 = n // 2

    @pl.when(it == 0)
    def _():
        _stage_inputs(x_ref, mid_ref, stage_ref, z_ref, 0)
        acc_ref[...] = jnp.zeros_like(acc_ref)

    m = m_ref[...]
    pq = _dot(m, x_ref[...])
    tw = [_wide(tw_ref[j], tn) for j in range(8)]
    zs = _spectrum_rows(pq, tw, tn)
    ys = [(zr * k_ref[2 * j] - zi * k_ref[2 * j + 1], zr * k_ref[2 * j + 1] + zi * k_ref[2 * j])
          for j, (zr, zi) in enumerate(zs)]
    (y1r, y1i), (y2r, y2i), (y3r, y3i), (y4r, y4i) = ys
    c1, s1, c2, s2 = tw[0], tw[1], tw[2], tw[3]
    dr, di = y1r - y4r, y1i + y4i
    er, ei = y2r - y3r, y2i + y3i
    blocks = [(y1r + y4r, -(y1i - y4i)),
              (y2r + y3r, y2i - y3i),
              (c1 * dr - s1 * di, -(c1 * di + s1 * dr)),
              (c2 * er - s2 * ei, c2 * ei + s2 * er)]
    g = jnp.concatenate([jnp.concatenate([a, b], axis=0) for a, b in blocks], axis=1).astype(BF)
    acc_ref[...] += lax.dot_general(m, g, _TN, preferred_element_type=F32)

    @pl.when(it == pl.num_programs(2) - 1)
    def _():
        cm, sm, sign = _quarter_patterns(l2)
        (zhr, zhi), (z3r, z3i) = _mid_spectrum(mid_ref[0:1, :], mid_ref[1:2, :], mid_ref[2:3, :], mid_ref[3:4, :])
        khr, khi, k3r, k3i = kmid_ref[0:1, :], kmid_ref[1:2, :], kmid_ref[2:3, :], kmid_ref[3:4, :]
        yhr, yhi = zhr * khr - zhi * khi, zhr * khi + zhi * khr
        ytr, yti = z3r * k3r - z3i * k3i, z3r * k3i + z3i * k3r
        fr, fi = yhr - ytr, yhi + yti
        g0r, g0i = yhr + ytr, yhi - yti
        g1r, g1i = _RT2 * (fr - fi), _RT2 * (fi + fr)
        ye = acc_ref[:, 0:tn] + sign * acc_ref[:, tn:2 * tn] + cm * g0r - sm * g0i
        yo = acc_ref[:, 2 * tn:3 * tn] + sign * acc_ref[:, 3 * tn:4 * tn] + cm * g1r - sm * g1i
        for s in range(tn // LANES):
            lanes = slice(s * LANES, (s + 1) * LANES)
            stage_ref[s, pl.ds(0, l2, stride=2), :] = ye[:, lanes]
            stage_ref[s, pl.ds(1, l2, stride=2), :] = yo[:, lanes]
            y_ref[:, lanes] = stage_ref[s].astype(y_ref.dtype)


def _fftconv_call(z, dft, tw, kf, kmid, batch, tk):
    t, wd = z.shape
    n = t // batch
    l2 = n // 2
    half = l2 // 2
    tn = _tile(wd, 256, LANES)
    z_spec = pl.BlockSpec((n, tn), lambda b, c, i: (b, c))
    return pl.pallas_call(
        _fftconv_kernel,
        name="fftconv",
        grid=(batch, wd // tn, half // tk),
        in_specs=[z_spec,
                  pl.BlockSpec((2 * tk, l2), lambda b, c, i: (i, 0)),
                  pl.BlockSpec((8, tk, LANES), lambda b, c, i: (0, i, 0)),
                  pl.BlockSpec((8, tk, tn), lambda b, c, i: (0, i, c)),
                  pl.BlockSpec((4, tn), lambda b, c, i: (0, c))],
        out_specs=z_spec,
        out_shape=jax.ShapeDtypeStruct((t, wd), BF),
        scratch_shapes=[pltpu.VMEM((l2, 4 * tn), BF), pltpu.VMEM((l2, 4 * tn), F32), pltpu.VMEM((8, tn), F32),
                        pltpu.VMEM((tn // LANES, n, LANES), F32)],
        compiler_params=_params("parallel", "parallel", "arbitrary"),
    )(z, dft, tw, kf, kmid)


def _mix_kernel(y_ref, x0_ref, z_ref, hbias_ref, of_ref, ob_ref, gz_ref, gn_ref, ghy_ref, ggdn_ref,
                why_ref, wgdn_ref, m_ref, *, dv, row_blocks):
    hbias, gn = hbias_ref[...], gn_ref[...]
    rb = m_ref.shape[0] // row_blocks
    ops = []
    for r in range(row_blocks):
        rs = slice(r * rb, (r + 1) * rb)
        z = z_ref[rs, :].astype(F32)
        uhy = (x0_ref[rs, :].astype(F32) * (y_ref[rs, :].astype(F32) + z * hbias)).astype(BF)
        parts = []
        for s in range(0, of_ref.shape[1], dv):
            o = of_ref[rs, s:s + dv].astype(F32) + ob_ref[rs, s:s + dv].astype(F32)
            on = o * lax.rsqrt(jnp.mean(o * o, axis=-1, keepdims=True) + EPS) * gn
            parts.append((on * gz_ref[rs, s:s + dv].astype(F32)).astype(BF))
        ops.append((rs, uhy, jnp.concatenate(parts, axis=1)))
    def epilogue(op, acc):
        rs = op[0]
        m_ref[rs, :] = (ghy_ref[rs, :].astype(F32) * acc[0] + ggdn_ref[rs, :].astype(F32) * acc[1]).astype(m_ref.dtype)

    _pipelined(ops, lambda op: (_dot(op[1], why_ref[...]), _dot(op[2], wgdn_ref[...])), epilogue)


def _mix_call(y, x0, z, hbias, o_f, o_b, gz, gn, gates, w_hy, w_gdn, dv):
    t, wd = y.shape
    hd = o_f.shape[1]
    d = w_hy.shape[1]
    tm = _tile(t, 256, 32)
    row = lambda c: pl.BlockSpec((tm, c), lambda i: (i, 0))
    return pl.pallas_call(
        functools.partial(_mix_kernel, dv=dv, row_blocks=2),
        name="mix",
        grid=(t // tm,),
        in_specs=[row(wd), row(wd), row(wd), pl.BlockSpec((1, wd), lambda i: (0, 0)),
                  row(hd), row(hd), row(hd), pl.BlockSpec((1, dv), lambda i: (0, 0)),
                  pl.BlockSpec((tm, d), lambda i: (i, 0)), pl.BlockSpec((tm, d), lambda i: (i, 1)),
                  _resident((wd, d)), _resident((hd, d))],
        out_specs=pl.BlockSpec((tm, d), lambda i: (i, 0)),
        out_shape=jax.ShapeDtypeStruct((t, d), BF),
        compiler_params=_params("parallel"),
    )(y, x0, z, hbias, o_f, o_b, gz, gn, gates, gates, w_hy, w_gdn)


def _resid_kernel(m_ref, w_ref, x_ref, ga_ref, o_ref):
    o_ref[...] = x_ref[...] + ga_ref[0] * _dot(m_ref[...], w_ref[...])


def _resid_call(m, w, x2, mod3, sec_ga, rows_per_mod):
    t, d = m.shape
    n = w.shape[1]
    tm = _tile(rows_per_mod, 512, 16)
    per = rows_per_mod // tm
    return pl.pallas_call(
        _resid_kernel,
        name="out_resid",
        grid=(t // tm,),
        in_specs=[pl.BlockSpec((tm, d), lambda i: (i, 0)),
                  _resident((d, n)),
                  pl.BlockSpec((tm, n), lambda i: (i, 0)),
                  pl.BlockSpec((1, 1, n), lambda i: (i // per, 0, sec_ga))],
        out_specs=pl.BlockSpec((tm, n), lambda i: (i, 0)),
        out_shape=jax.ShapeDtypeStruct((t, n), F32),
        compiler_params=_params("parallel"),
    )(m, w, x2, mod3)


def _ffn_up_kernel(x_ref, g_ref, sc_ref, sh_ref, wg_ref, wu_ref, o_ref, h_ref):
    @pl.when(pl.program_id(1) == 0)
    def _():
        x = x_ref[...]
        xn = x * lax.rsqrt(jnp.mean(x * x, axis=-1, keepdims=True) + EPS)
        h_ref[...] = ((xn * g_ref[...]) * (1.0 + sc_ref[0]) + sh_ref[0]).astype(BF)

    def matmul(b):
        h = h_ref[b[0], :]
        return _dot(h, wg_ref[:, b[1]]), _dot(h, wu_ref[:, b[1]])

    def epilogue(b, acc):
        o_ref[b[0], b[1]] = (_silu(acc[0]) * acc[1]).astype(o_ref.dtype)

    _pipelined(_sub_blocks(*o_ref.shape), matmul, epilogue)


def _ffn_up_call(x2, g, mod3, sec_sh, sec_sc, w_up, rows_per_mod):
    t, d = x2.shape
    ff = w_up.shape[1] // 2
    tm = _tile(rows_per_mod, ROW_TILE, 16)
    tn = _tile(ff, 512, LANES)
    nb = ff // tn
    per = rows_per_mod // tm
    mod_map = lambda sec: (lambda i, j: (i // per, 0, sec))
    return pl.pallas_call(
        _ffn_up_kernel,
        name="ffn_up",
        grid=(t // tm, nb),
        in_specs=[pl.BlockSpec((tm, d), lambda i, j: (i, 0)),
                  pl.BlockSpec((1, d), lambda i, j: (0, 0)),
                  pl.BlockSpec((1, 1, d), mod_map(sec_sc)),
                  pl.BlockSpec((1, 1, d), mod_map(sec_sh)),
                  pl.BlockSpec((d, tn), lambda i, j: (0, j)),
                  pl.BlockSpec((d, tn), lambda i, j: (0, nb + j))],
        out_specs=pl.BlockSpec((tm, tn), lambda i, j: (i, j)),
        out_shape=jax.ShapeDtypeStruct((t, ff), BF),
        scratch_shapes=[pltpu.VMEM((tm, d), BF)],
        compiler_params=_params("parallel", "arbitrary"),
    )(x2, g, mod3, mod3, w_up, w_up)


def _ffn_down_kernel(a_ref, w_ref, x_ref, ga_ref, gf_ref, o_ref):
    x = x_ref[...] + ga_ref[0] * _dot(a_ref[...], w_ref[...])
    o_ref[...] = x * lax.rsqrt(jnp.mean(x * x, axis=-1, keepdims=True) + EPS) * gf_ref[...]


def _ffn_down_call(a, w, x2, mod3, sec_ga, g_final, rows_per_mod):
    t, ff = a.shape
    d = w.shape[1]
    tm = _tile(rows_per_mod, 512, 16)
    per = rows_per_mod // tm
    return pl.pallas_call(
        _ffn_down_kernel,
        name="ffn_down",
        grid=(t // tm,),
        in_specs=[pl.BlockSpec((tm, ff), lambda i: (i, 0)),
                  _resident((ff, d)),
                  pl.BlockSpec((tm, d), lambda i: (i, 0)),
                  pl.BlockSpec((1, 1, d), lambda i: (i // per, 0, sec_ga)),
                  pl.BlockSpec((1, d), lambda i: (0, 0))],
        out_specs=pl.BlockSpec((tm, d), lambda i: (i, 0)),
        out_shape=jax.ShapeDtypeStruct((t, d), F32),
        compiler_params=_params("parallel"),
    )(a, w, x2, mod3, g_final)


def _group_columns(a, heads, hb):
    g = heads // hb
    lead = a.shape[:-1]
    a = a.reshape(*lead, 4, g, hb)
    a = jnp.moveaxis(a, -2, -3).reshape(*lead, g, 4 * hb)
    a = jnp.pad(a, [(0, 0)] * (len(lead) + 1) + [(0, LANES - 4 * hb)])
    return a.reshape(*lead, g * LANES)


def _layer(x, c, ctx, c_ctx, w_ada, b_ada, norm_mix, norm_ffn, w_in, hy_conv, hy_bias,
           hy_fw1, hy_fb1, hy_fw2, hy_fb2, hy_fw3, hy_fb3, hy_fout, hy_freq,
           gdn_conv, gdn_a_log, gdn_dt_bias, gdn_norm, w_hy_out, w_gdn_out, w_o, w_up, w_down,
           norm_final):
    b, n, d = x.shape
    n_ctx = ctx.shape[1]
    wd = hy_bias.shape[-1]
    heads = gdn_a_log.shape[-1]
    dv = gdn_norm.shape[-1]
    d_v = heads * dv
    d_qk = (gdn_conv.shape[-1] - d_v) // 2
    dk = d_qk // heads
    hb = min(16, heads)
    groups = heads // hb
    i_hy, i_qkv, i_z, i_sc = 3 * wd, 3 * wd + 2 * d_qk + d_v, 3 * wd + 2 * d_qk + 2 * d_v, 3 * wd + 2 * d_qk + 2 * d_v + 4 * heads

    rows = -(-(b + 1) // 8) * 8
    cvec = jnp.concatenate([jax.nn.silu(c), jax.nn.silu(c_ctx)[None], jnp.zeros((rows - b - 1, d), F32)], axis=0)
    mod = _ada_call(cvec.astype(BF), w_ada, b_ada.reshape(1, -1))
    mod3 = mod.reshape(rows, 1, 6 * d)

    x2 = x.reshape(b * n, d)
    ctx2 = ctx.reshape(b * n_ctx, d)
    g_mix = norm_mix.reshape(1, d)
    h_lat = _normmod_call(x2, g_mix, mod3, 0, 1, n, 0)
    h_ctx = _normmod_call(ctx2, g_mix, mod3, 0, 1, b * n_ctx, b)

    w_in16 = w_in.astype(BF)
    w_gate = w_in16[:, i_sc:]
    w_scal = _group_columns(w_in[:, i_z:i_sc], heads, hb).astype(BF)
    a_vec = _group_columns(jnp.concatenate([jnp.zeros((2 * heads,), F32), gdn_a_log.reshape(-1)]), heads, hb)[None]
    dt_vec = _group_columns(jnp.concatenate([jnp.zeros((2 * heads,), F32), gdn_dt_bias.reshape(-1)]), heads, hb)[None]

    def gdn_branch(h, period, s0f, s0b):
        t = h.shape[0]
        qk = _proj_qkv_call(h, w_in16, i_hy, gdn_conv, 0, 2 * d_qk, period, dk)
        v = _proj_qkv_call(h, w_in16, i_hy + 2 * d_qk, gdn_conv, 2 * d_qk, d_v, period, 0)
        scal = _scal_call(h, w_scal, a_vec, dt_vec, hb)
        scal_t = scal.reshape(t // CHUNK, CHUNK, groups, LANES).transpose(0, 2, 3, 1)
        return _gdn_call(qk, v, scal, scal_t, s0f, s0b, b, heads, dk, dv, hb)

    zero_state = jnp.zeros((b, heads, dk, dv), F32)
    _, _, s_f, s_b = gdn_branch(h_ctx, n_ctx, zero_state, zero_state)
    o_f, o_b, _, _ = gdn_branch(h_lat, GRID_W, s_f, s_b)

    x0, z = _proj_hy_call(h_lat, w_in16, 0, wd, hy_conv, GRID_W)
    tt = np.linspace(0.0, 1.0, n, dtype=np.float32)[:, None]
    bands = (hy_fw1.shape[0] - 1) // 2
    wv = (2.0 * math.pi * np.arange(n, dtype=np.float32)[:, None] / n).astype(np.float32)
    fv = np.linspace(1e-4, bands - 1, bands, dtype=np.float32)[None, :]
    zfeat = np.concatenate([tt, np.cos(fv * wv), -np.sin(fv * wv)], axis=-1).astype(np.float32)
    emb = zfeat.shape[1]
    emb_pad = -(-emb // 8) * 8
    zfeat = jnp.asarray(np.pad(zfeat, [(0, 0), (0, emb_pad - emb)]))
    fw1 = jnp.pad(hy_fw1, [(0, emb_pad - emb), (0, 0)])
    deltas = jnp.abs(jnp.linspace(HY_MIN_DECAY, HY_MAX_DECAY, wd, dtype=F32))[None]
    hs, hdiff = _filter_call(zfeat, fw1, hy_fb1[None], hy_fw2, hy_fb2[None], hy_fw3, hy_fb3[None],
                             hy_freq[None], hy_fout, jnp.asarray(tt), deltas)
    l2 = n // 2
    tk = _tile(l2 // 2, 256, 16)
    dft = _dftmat_call(l2, l2 // 2, tk)
    tw = _twiddles(l2 // 2, l2)
    kf, kmid = _filter_fft_call(hs, hdiff, dft, tw, tk)
    y = _fftconv_call(z, dft, tw, kf, kmid, b, tk)

    gz = _proj_act_call(h_lat, w_in16, i_qkv, d_v, _silu)
    gates = _proj_act_call(h_lat, w_gate, 0, 2 * d, _sigmoid)
    m = _mix_call(y, x0, z, hy_bias[None], o_f, o_b, gz, gdn_norm[None], gates,
                  w_hy_out.astype(BF), w_gdn_out.astype(BF), dv)
    x1 = _resid_call(m, w_o.astype(BF), x2, mod3, 2, n)

    a = _ffn_up_call(x1, norm_ffn.reshape(1, d), mod3, 3, 4, w_up.astype(BF), n)
    out = _ffn_down_call(a, w_down.astype(BF), x1, mod3, 5, norm_final.reshape(1, d), n)
    return out.reshape(b, n, d)


def kernel(x, c, ctx, c_ctx, w_ada, b_ada, norm_mix, norm_ffn, w_in, hy_conv, hy_bias, hy_fw1, hy_fb1, hy_fw2, hy_fb2, hy_fw3, hy_fb3, hy_fout, hy_freq, gdn_conv, gdn_a_log, gdn_dt_bias, gdn_norm, w_hy_out, w_gdn_out, w_o, w_up, w_down, norm_final):
    assert w_ada.shape[0] == 1, "single-layer block: the context stream is only read"
    return _layer(x, c, ctx, c_ctx, w_ada[0], b_ada[0], norm_mix[0], norm_ffn[0], w_in[0], hy_conv[0],
                  hy_bias[0], hy_fw1[0], hy_fb1[0], hy_fw2[0], hy_fb2[0], hy_fw3[0], hy_fb3[0],
                  hy_fout[0], hy_freq[0], gdn_conv[0], gdn_a_log[0], gdn_dt_bias[0], gdn_norm[0],
                  w_hy_out[0], w_gdn_out[0], w_o[0], w_up[0], w_down[0], norm_final)
```

```python
import functools
import math

import jax
import jax.numpy as jnp
import numpy as np
from jax import lax
from jax.experimental import pallas as pl
from jax.experimental.pallas import tpu as pltpu

BF = jnp.bfloat16
F32 = jnp.float32

GRID_W = 64
CHUNK = 64
SUB = 16
GDN_CHUNKS_PER_STEP = 1
EPS = 1e-6
HY_MIN_DECAY = math.log(1e-2) / 1.5
HY_MAX_DECAY = math.log(1e-2) / 0.3
LANES = 128
VMEM_LIMIT = 56 * 1024 * 1024
ROW_TILE = 1024

_NT = (((1,), (1,)), ((), ()))
_TN = (((0,), (0,)), ((), ()))


def _tile(n, pref, mult):
    t = min(pref, n)
    t -= t % mult
    while t >= mult:
        if n % t == 0:
            return t
        t -= mult
    return n


def _params(*sem):
    return pltpu.CompilerParams(dimension_semantics=sem, vmem_limit_bytes=VMEM_LIMIT)


def _resident(shape):
    return pl.BlockSpec(shape, lambda *_: (0,) * len(shape), pipeline_mode=pl.Buffered(1))


def _dot(a, b):
    return jnp.dot(a, b, preferred_element_type=F32)


def _split3(x):
    hi = x.astype(BF)
    r = x - hi.astype(F32)
    mid = r.astype(BF)
    lo = (r - mid.astype(F32)).astype(BF)
    return hi, mid, lo


def _dot_f32(a, b):
    ah, am, _ = _split3(a)
    bh, bm, _ = _split3(b)
    return _dot(ah, bh) + (_dot(ah, bm) + _dot(am, bh))


def _silu(x):
    return x * (1.0 / (1.0 + jnp.exp(-x)))


def _sigmoid(x):
    return 1.0 / (1.0 + jnp.exp(-x))


def _ada_kernel(c_ref, w_ref, b_ref, o_ref):
    o_ref[...] = _dot(c_ref[...], w_ref[...].astype(BF)) + b_ref[...]


def _ada_call(cvec, w, b):
    d, n = w.shape
    r = cvec.shape[0]
    tn = _tile(n, 1024, LANES)
    return pl.pallas_call(
        _ada_kernel,
        name="ada",
        grid=(n // tn,),
        in_specs=[pl.BlockSpec((r, d), lambda j: (0, 0)),
                  pl.BlockSpec((d, tn), lambda j: (0, j)),
                  pl.BlockSpec((1, tn), lambda j: (0, j))],
        out_specs=pl.BlockSpec((r, tn), lambda j: (0, j)),
        out_shape=jax.ShapeDtypeStruct((r, n), F32),
        compiler_params=_params("parallel"),
    )(cvec, w, b)


def _normmod_kernel(x_ref, g_ref, sc_ref, sh_ref, o_ref):
    x = x_ref[...]
    xn = x * lax.rsqrt(jnp.mean(x * x, axis=-1, keepdims=True) + EPS)
    o_ref[...] = ((xn * g_ref[...]) * (1.0 + sc_ref[0]) + sh_ref[0]).astype(o_ref.dtype)


def _normmod_call(x2, g, mod3, sec_sh, sec_sc, rows_per_mod, mod_row0):
    t, d = x2.shape
    tm = _tile(rows_per_mod, 512, 8)
    per = rows_per_mod // tm
    mod_map = lambda sec: (lambda i: (mod_row0 + i // per, 0, sec))
    return pl.pallas_call(
        _normmod_kernel,
        name="normmod",
        grid=(t // tm,),
        in_specs=[pl.BlockSpec((tm, d), lambda i: (i, 0)),
                  pl.BlockSpec((1, d), lambda i: (0, 0)),
                  pl.BlockSpec((1, 1, d), mod_map(sec_sc)),
                  pl.BlockSpec((1, 1, d), mod_map(sec_sh))],
        out_specs=pl.BlockSpec((tm, d), lambda i: (i, 0)),
        out_shape=jax.ShapeDtypeStruct((t, d), BF),
        compiler_params=_params("parallel"),
    )(x2, g, mod3, mod3)


MXU_COLS = 256


ROW_SUB = 128


def _col_blocks(n):
    sub = MXU_COLS if n % MXU_COLS == 0 else n
    return [slice(s, s + sub) for s in range(0, n, sub)]


def _row_sub(rows, mult, pref=ROW_SUB):
    r = -(-pref // mult) * mult
    while r < rows:
        if rows % r == 0:
            return r
        r += mult
    return rows


def _sub_blocks(rows, cols, rsub):
    return [(slice(r, r + rsub), cs) for cs in _col_blocks(cols) for r in range(0, rows, rsub)]


STAGE_PAD = 8


def _stage_scratch(rsub, cols, n):
    cs = _col_blocks(cols)[0]
    return pltpu.VMEM((2, n, rsub + 2 * STAGE_PAD, cs.stop - cs.start), F32)


def _stage(stage_ref, slot, j, value):
    rows = value.shape[0]
    guard = jnp.zeros((1, value.shape[1]), F32)
    stage_ref[slot, j, STAGE_PAD - 1:STAGE_PAD, :] = guard
    stage_ref[slot, j, STAGE_PAD:STAGE_PAD + rows, :] = value
    stage_ref[slot, j, STAGE_PAD + rows:STAGE_PAD + rows + 1, :] = guard


def _conv3_staged(stage_ref, slot, j, rows, cw, period):
    row = lax.broadcasted_iota(jnp.int32, (rows, 1), 0) % period
    p = stage_ref[slot, j, STAGE_PAD:STAGE_PAD + rows, :]
    prev = jnp.where(row == 0, 0.0, stage_ref[slot, j, STAGE_PAD - 1:STAGE_PAD - 1 + rows, :])
    nxt = jnp.where(row == period - 1, 0.0, stage_ref[slot, j, STAGE_PAD + 1:STAGE_PAD + 1 + rows, :])
    return prev * cw[0:1, :] + p * cw[1:2, :] + nxt * cw[2:3, :]


def _pipelined(blocks, matmul, epilogue):
    pending = None
    for b in blocks:
        acc = matmul(b)
        if pending is not None:
            epilogue(*pending)
        pending = (b, acc)
    epilogue(*pending)


def _staged(blocks, matmul, epilogue):
    pending = None
    for k, b in enumerate(blocks):
        matmul(b, k % 2)
        if pending is not None:
            epilogue(*pending)
        pending = (b, k % 2)
    epilogue(*pending)


def _proj_hy_kernel(h_ref, w0_ref, w1_ref, w2_ref, c0_ref, c1_ref, c2_ref, x0_ref, z_ref, stage_ref,
                    *, period, rows):
    blocks = _sub_blocks(*x0_ref.shape, rows)

    def matmul(b, slot):
        h = h_ref[b[0], :]
        for j, w_ref in enumerate((w0_ref, w1_ref, w2_ref)):
            _stage(stage_ref, slot, j, _dot(h, w_ref[:, b[1]]))

    def epilogue(b, slot):
        rs, sl = b
        x0, x1, v = (_conv3_staged(stage_ref, slot, j, rows, c_ref[:, sl], period)
                     for j, c_ref in enumerate((c0_ref, c1_ref, c2_ref)))
        x0_ref[rs, sl] = x0.astype(x0_ref.dtype)
        z_ref[rs, sl] = (x1 * v).astype(z_ref.dtype)

    _staged(blocks, matmul, epilogue)


def _proj_hy_call(h, w, w_off, wd, cw, period):
    t, d = h.shape
    tm = _tile(t, ROW_TILE, period)
    tn = _tile(math.gcd(wd, w_off) if w_off else wd, 512, LANES)
    rsub = _row_sub(tm, period, 2 * ROW_SUB)
    nb = wd // tn
    off = w_off // tn
    wspec = lambda s: pl.BlockSpec((d, tn), lambda j, i: (0, off + s * nb + j))
    cspec = lambda s: pl.BlockSpec((3, tn), lambda j, i: (0, s * nb + j))
    ospec = pl.BlockSpec((tm, tn), lambda j, i: (i, j))
    return pl.pallas_call(
        functools.partial(_proj_hy_kernel, period=period, rows=rsub),
        name="proj_hy",
        grid=(nb, t // tm),
        in_specs=[pl.BlockSpec((tm, d), lambda j, i: (i, 0)), wspec(0), wspec(1), wspec(2),
                  cspec(0), cspec(1), cspec(2)],
        out_specs=[ospec, ospec],
        out_shape=[jax.ShapeDtypeStruct((t, wd), BF)] * 2,
        scratch_shapes=[_stage_scratch(rsub, tn, 3)],
        compiler_params=_params("parallel", "arbitrary"),
    )(h, w, w, w, cw, cw, cw)


def _proj_qkv_kernel(h_ref, w_ref, c_ref, o_ref, stage_ref, *, period, norm_dim, rows):
    blocks = _sub_blocks(*o_ref.shape, rows)

    def matmul(b, slot):
        _stage(stage_ref, slot, 0, _dot(h_ref[b[0], :], w_ref[:, b[1]]))

    def epilogue(b, slot):
        rs, sl = b
        u = _silu(_conv3_staged(stage_ref, slot, 0, rows, c_ref[:, sl], period))
        if norm_dim:
            parts = []
            for s in range(0, u.shape[1], norm_dim):
                blk = u[:, s:s + norm_dim]
                parts.append(blk * lax.rsqrt(jnp.sum(blk * blk, axis=-1, keepdims=True) + EPS))
            u = jnp.concatenate(parts, axis=1)
        o_ref[rs, sl] = u.astype(o_ref.dtype)

    _staged(blocks, matmul, epilogue)


def _proj_qkv_call(h, w, w_off, cw, c_off, n, period, norm_dim):
    t, d = h.shape
    tm = _tile(t, ROW_TILE, period)
    tn = _tile(math.gcd(math.gcd(n, w_off), c_off) if (w_off or c_off) else n, 512, LANES)
    rsub = _row_sub(tm, period)
    woff, coff = w_off // tn, c_off // tn
    return pl.pallas_call(
        functools.partial(_proj_qkv_kernel, period=period, norm_dim=norm_dim, rows=rsub),
        name="proj_qk" if norm_dim else "proj_v",
        grid=(n // tn, t // tm),
        in_specs=[pl.BlockSpec((tm, d), lambda j, i: (i, 0)),
                  pl.BlockSpec((d, tn), lambda j, i: (0, woff + j)),
                  pl.BlockSpec((3, tn), lambda j, i: (0, coff + j))],
        out_specs=pl.BlockSpec((tm, tn), lambda j, i: (i, j)),
        out_shape=jax.ShapeDtypeStruct((t, n), BF),
        scratch_shapes=[_stage_scratch(rsub, tn, 1)],
        compiler_params=_params("parallel", "arbitrary"),
    )(h, w, cw)


def _proj_act_kernel(h_ref, w_ref, o_ref, *, act):
    def epilogue(b, acc):
        o_ref[b[0], b[1]] = act(acc).astype(o_ref.dtype)

    blocks = _sub_blocks(*o_ref.shape, _row_sub(o_ref.shape[0], 16))
    _pipelined(blocks, lambda b: _dot(h_ref[b[0], :], w_ref[:, b[1]]), epilogue)


def _proj_act_call(h, w, w_off, n, act):
    t, d = h.shape
    tm = _tile(t, ROW_TILE, 16)
    tn = _tile(math.gcd(n, w_off) if w_off else n, 1024, LANES)
    woff = w_off // tn
    return pl.pallas_call(
        functools.partial(_proj_act_kernel, act=act),
        name="proj_act",
        grid=(n // tn, t // tm),
        in_specs=[pl.BlockSpec((tm, d), lambda j, i: (i, 0)),
                  pl.BlockSpec((d, tn), lambda j, i: (0, woff + j))],
        out_specs=pl.BlockSpec((tm, tn), lambda j, i: (i, j)),
        out_shape=jax.ShapeDtypeStruct((t, n), BF),
        compiler_params=_params("parallel", "arbitrary"),
    )(h, w)


def _scal_kernel(h_ref, w_ref, a_ref, dt_ref, o_ref, *, hb):
    s = _dot(h_ref[...], w_ref[...])
    tm = s.shape[0]
    lane = lax.broadcasted_iota(jnp.int32, (1, s.shape[1]), 1) % LANES
    beta = _sigmoid(s)
    xs = s + dt_ref[...]
    softplus = jnp.maximum(xs, 0.0) + jnp.log(1.0 + jnp.exp(-jnp.abs(xs)))
    g = jnp.where((lane >= 2 * hb) & (lane < 4 * hb), -jnp.exp(a_ref[...]) * softplus, 0.0)
    ri = lax.broadcasted_iota(jnp.int32, (tm, tm), 0)
    ci = lax.broadcasted_iota(jnp.int32, (tm, tm), 1)
    same = (ri // CHUNK) == (ci // CHUNK)
    lo = jnp.where(same & (ci <= ri), 1.0, 0.0).astype(BF)
    up = jnp.where(same & (ci >= ri), 1.0, 0.0).astype(BF)
    g0, g1, g2 = _split3(g)
    pre = _dot(lo, g0) + (_dot(lo, g1) + _dot(lo, g2))
    suf = _dot(up, g0) + (_dot(up, g1) + _dot(up, g2))
    o_ref[...] = jnp.where(lane < 2 * hb, beta, jnp.where(lane < 3 * hb, pre, jnp.where(lane < 4 * hb, suf, 0.0)))


def _scal_call(h, w, a_vec, dt_vec, hb):
    t, d = h.shape
    n = w.shape[1]
    tm = _tile(t, 256, CHUNK)
    return pl.pallas_call(
        functools.partial(_scal_kernel, hb=hb),
        name="scal",
        grid=(t // tm,),
        in_specs=[pl.BlockSpec((tm, d), lambda i: (i, 0)),
                  pl.BlockSpec((d, n), lambda i: (0, 0)),
                  pl.BlockSpec((1, n), lambda i: (0, 0)),
                  pl.BlockSpec((1, n), lambda i: (0, 0))],
        out_specs=pl.BlockSpec((tm, n), lambda i: (i, 0)),
        out_shape=jax.ShapeDtypeStruct((t, n), F32),
        compiler_params=_params("parallel"),
    )(h, w, a_vec, dt_vec)


def _tri_inverse_all(a_list):
    c = a_list[0].shape[0]
    ri = lax.broadcasted_iota(jnp.int32, (c, c), 0)
    ci = lax.broadcasted_iota(jnp.int32, (c, c), 1)
    same = (ri // SUB) == (ci // SUB)
    rs = lax.broadcasted_iota(jnp.int32, (SUB, c), 0)
    cs = lax.broadcasted_iota(jnp.int32, (SUB, c), 1)
    eye = jnp.where(rs == cs % SUB, 1.0, 0.0)

    def block_diag(xc):
        return jnp.where(same, jnp.concatenate([xc] * (c // SUB), axis=0), 0.0)

    xs = []
    for a in a_list:
        neg = jnp.where(same, -a, 0.0)
        xs.append(sum(neg[s:s + SUB] for s in range(0, c, SUB)))
    ps = [eye + x for x in xs]
    for _ in range(int(math.log2(SUB)) - 1):
        xbd = [block_diag(x).astype(BF) for x in xs]
        xs = [_dot(x.astype(BF), w) for x, w in zip(xs, xbd)]
        xbd = [block_diag(x).astype(BF) for x in xs]
        incs = [_dot(p.astype(BF), w) for p, w in zip(ps, xbd)]
        ps = [p + i for p, i in zip(ps, incs)]
    ps = [block_diag(p) for p in ps]
    size = SUB
    while size < c:
        sel = ((ri // (2 * size)) == (ci // (2 * size))) & ((ri // size) != (ci // size))
        p16 = [p.astype(BF) for p in ps]
        po = [_dot(p, jnp.where(sel, a, 0.0).astype(BF)).astype(BF) for p, a in zip(p16, a_list)]
        cor = [_dot(x, p) for x, p in zip(po, p16)]
        ps = [p - x for p, x in zip(ps, cor)]
        size *= 2
    return ps


def _gdn_prepare(chains, scale):
    c = chains[0][0].shape[0]
    dv = chains[0][2].shape[1]
    ri = lax.broadcasted_iota(jnp.int32, (c, c), 0)
    ci = lax.broadcasted_iota(jnp.int32, (c, c), 1)
    pre = []
    for q, k, v, sc, sct, lower, col_beta, col_g in chains:
        beta = sc[:, col_beta:col_beta + 1]
        gcol = sc[:, col_g:col_g + 1]
        grow = sct[col_g:col_g + 1, :]
        incl = (ri >= ci) if lower else (ri <= ci)
        strict = (ri > ci) if lower else (ri < ci)
        dec = jnp.exp(jnp.where(incl, gcol - grow, -jnp.inf))
        glast = gcol[c - 1:c, :] if lower else gcol[0:1, :]
        eg = jnp.exp(gcol)
        kf = k.astype(F32)
        kb = kf * beta
        rhs = jnp.concatenate([v.astype(F32) * beta, kb * eg], axis=1).astype(BF)
        qdec = (q.astype(F32) * (eg * scale)).astype(BF)
        kdec = (kf * jnp.exp(glast - gcol)).astype(BF)
        pre.append((incl, strict, dec, kb.astype(BF), rhs, qdec, kdec, jnp.exp(glast)))
    kq = [lax.dot_general(jnp.concatenate([p[3], ch[0]], axis=0), ch[1], _NT, preferred_element_type=F32)
          for p, ch in zip(pre, chains)]
    a = [jnp.where(p[1], x[0:c] * p[2], 0.0) for p, x in zip(pre, kq)]
    qkm = [(jnp.where(p[0], x[c:2 * c] * p[2], 0.0) * scale).astype(BF) for p, x in zip(pre, kq)]
    t = _tri_inverse_all(a)
    sol = [_dot(x.astype(BF), p[4]) for x, p in zip(t, pre)]
    return [(x[:, :dv], x[:, dv:].astype(BF), m, p[5], p[6], p[7]) for x, m, p in zip(sol, qkm, pre)]


def _gdn_advance(prep, states):
    c = prep[0][0].shape[0]
    s16 = [s.astype(BF) for s in states]
    wq = [_dot(jnp.concatenate([p[1], p[3]], axis=0), s) for p, s in zip(prep, s16)]
    vn = [(p[0] - x[0:c]).astype(BF) for p, x in zip(prep, wq)]
    o1 = [x[c:2 * c] for x in wq]
    o2 = [_dot(p[2], y) for p, y in zip(prep, vn)]
    ds = [lax.dot_general(p[4], y, _TN, preferred_element_type=F32) for p, y in zip(prep, vn)]
    return [x + y for x, y in zip(o1, o2)], [s * p[5] + z for s, p, z in zip(states, prep, ds)]


def _gdn_kernel(qf_ref, kf_ref, vf_ref, qb_ref, kb_ref, vb_ref, scf_ref, scb_ref, stf_ref, stb_ref,
                s0f_ref, s0b_ref, of_ref, ob_ref, sf_ref, sb_ref, st_ref, *, hb, dk, dv, scale):
    n = pl.program_id(2)

    @pl.when(n == 0)
    def _():
        st_ref[0] = s0f_ref[0]
        st_ref[1] = s0b_ref[0]

    sub = qf_ref.shape[0] // CHUNK
    chains, rows = [], []
    for a in range(sub):
        rf = slice(a * CHUNK, (a + 1) * CHUNK)
        rb = slice((sub - 1 - a) * CHUNK, (sub - a) * CHUNK)
        rows.append((rf, rb))
        for j in range(hb):
            qs, vs = slice(j * dk, (j + 1) * dk), slice(j * dv, (j + 1) * dv)
            chains.append((qf_ref[rf, qs], kf_ref[rf, qs], vf_ref[rf, vs], scf_ref[rf, :], stf_ref[a, 0],
                           True, j, 2 * hb + j))
            chains.append((qb_ref[rb, qs], kb_ref[rb, qs], vb_ref[rb, vs], scb_ref[rb, :],
                           stb_ref[sub - 1 - a, 0], False, hb + j, 3 * hb + j))
    prep = _gdn_prepare(chains, scale)
    states = [st_ref[d, j] for j in range(hb) for d in range(2)]
    for a, (rf, rb) in enumerate(rows):
        outs, states = _gdn_advance(prep[a * 2 * hb:(a + 1) * 2 * hb], states)
        for j in range(hb):
            vs = slice(j * dv, (j + 1) * dv)
            of_ref[rf, vs] = outs[2 * j].astype(of_ref.dtype)
            ob_ref[rb, vs] = outs[2 * j + 1].astype(ob_ref.dtype)
    for j in range(hb):
        st_ref[0, j] = states[2 * j]
        st_ref[1, j] = states[2 * j + 1]

    @pl.when(n == pl.num_programs(2) - 1)
    def _():
        sf_ref[0] = st_ref[0]
        sb_ref[0] = st_ref[1]


def _gdn_call(qk, v, scal, scal_t, s0f, s0b, batch, heads, dk, dv, hb):
    t = qk.shape[0]
    sub = GDN_CHUNKS_PER_STEP if (t // batch // CHUNK) % GDN_CHUNKS_PER_STEP == 0 else 1
    rows = sub * CHUNK
    nc = t // batch // rows
    g = heads // hb
    qb, vb = hb * dk, hb * dv
    fwd = lambda off: (lambda b, h, n: (b * nc + n, off + h))
    bwd = lambda off: (lambda b, h, n: (b * nc + nc - 1 - n, off + h))
    k_off = heads * dk // qb
    st_spec = pl.BlockSpec((1, hb, dk, dv), lambda b, h, n: (b, h, 0, 0))
    in_specs = [pl.BlockSpec((rows, qb), fwd(0)), pl.BlockSpec((rows, qb), fwd(k_off)),
                pl.BlockSpec((rows, vb), fwd(0)),
                pl.BlockSpec((rows, qb), bwd(0)), pl.BlockSpec((rows, qb), bwd(k_off)),
                pl.BlockSpec((rows, vb), bwd(0)),
                pl.BlockSpec((rows, LANES), fwd(0)), pl.BlockSpec((rows, LANES), bwd(0)),
                pl.BlockSpec((sub, 1, LANES, CHUNK), lambda b, h, n: (b * nc + n, h, 0, 0)),
                pl.BlockSpec((sub, 1, LANES, CHUNK), lambda b, h, n: (b * nc + nc - 1 - n, h, 0, 0)),
                st_spec, st_spec]
    out_specs = [pl.BlockSpec((rows, vb), fwd(0)), pl.BlockSpec((rows, vb), bwd(0)), st_spec, st_spec]
    o_sds = jax.ShapeDtypeStruct((t, heads * dv), BF)
    s_sds = jax.ShapeDtypeStruct((batch, heads, dk, dv), F32)
    return pl.pallas_call(
        functools.partial(_gdn_kernel, hb=hb, dk=dk, dv=dv, scale=dk ** -0.5),
        name="gdn",
        grid=(batch, g, nc),
        in_specs=in_specs,
        out_specs=out_specs,
        out_shape=[o_sds, o_sds, s_sds, s_sds],
        scratch_shapes=[pltpu.VMEM((2, hb, dk, dv), F32)],
        compiler_params=_params("parallel", "parallel", "arbitrary"),
    )(qk, qk, v, qk, qk, v, scal, scal, scal_t, scal_t, s0f, s0b)


def _filter_kernel(z_ref, w1_ref, b1_ref, w2_ref, b2_ref, w3_ref, b3_ref, fr_ref, ff_ref, fb_ref,
                   t_ref, dl_ref, hs_ref, hd_ref, h3_ref):
    @pl.when(pl.program_id(0) == 0)
    def _():
        fr = fr_ref[...]
        h = jnp.sin(fr * (_dot_f32(z_ref[...], w1_ref[...]) + b1_ref[...]))
        h = jnp.sin(fr * (_dot_f32(h, w2_ref[...]) + b2_ref[...]))
        h3_ref[...] = jnp.sin(fr * (_dot_f32(h, w3_ref[...]) + b3_ref[...]))

    h3 = h3_ref[...]
    n = h3.shape[0]
    dec = jnp.exp(-t_ref[...] * dl_ref[...])
    hf = _dot_f32(h3, ff_ref[...]) * dec
    row = lax.broadcasted_iota(jnp.int32, (n, 1), 0)
    hb = jnp.where(row == 0, 0.0, _dot_f32(h3, fb_ref[...]) * dec)
    hs_ref[...] = (hf + hb).astype(hs_ref.dtype)
    hd_ref[...] = (hb - hf).astype(hd_ref.dtype)


def _filter_call(zfeat, w1, b1, w2, b2, w3, b3, freq, fout, tcol, deltas):
    n, e = zfeat.shape
    o = w1.shape[1]
    wd = fout.shape[1] // 2
    tn = _tile(wd, 512, LANES)
    nb = wd // tn
    full = lambda a: pl.BlockSpec(a.shape, lambda j: (0,) * a.ndim)
    return pl.pallas_call(
        _filter_kernel,
        name="hyena_filter",
        grid=(nb,),
        in_specs=[full(zfeat), full(w1), full(b1), full(w2), full(b2), full(w3), full(b3), full(freq),
                  pl.BlockSpec((o, tn), lambda j: (0, j)), pl.BlockSpec((o, tn), lambda j: (0, nb + j)),
                  full(tcol), pl.BlockSpec((1, tn), lambda j: (0, j))],
        out_specs=[pl.BlockSpec((n, tn), lambda j: (0, j)), pl.BlockSpec((n, tn), lambda j: (0, j))],
        out_shape=[jax.ShapeDtypeStruct((n, wd), BF), jax.ShapeDtypeStruct((n, wd), BF)],
        scratch_shapes=[pltpu.VMEM((n, o), F32)],
        compiler_params=_params("arbitrary"),
    )(zfeat, w1, b1, w2, b2, w3, b3, freq, fout, fout, tcol, deltas)


def _dftmat_kernel(ca_ref, sa_ref, cb_ref, sb_ref, m_ref):
    tk = ca_ref.shape[0]
    cb, sb = cb_ref[...], sb_ref[...]
    for t1 in range(ca_ref.shape[1]):
        ca = ca_ref[:, t1:t1 + 1]
        sa = sa_ref[:, t1:t1 + 1]
        m_ref[0:tk, t1 * LANES:(t1 + 1) * LANES] = (ca * cb - sa * sb).astype(m_ref.dtype)
        m_ref[tk:2 * tk, t1 * LANES:(t1 + 1) * LANES] = (sa * cb + ca * sb).astype(m_ref.dtype)


def _dftmat_call(n, rows, tk):
    n1 = n // LANES
    k = jnp.arange(rows, dtype=jnp.int32)[:, None]
    ang_a = ((k * (LANES * jnp.arange(n1, dtype=jnp.int32)[None, :])) % (2 * n)).astype(F32) * (math.pi / n)
    ang_b = ((k * jnp.arange(LANES, dtype=jnp.int32)[None, :]) % (2 * n)).astype(F32) * (math.pi / n)
    a_spec = pl.BlockSpec((tk, n1), lambda i: (i, 0))
    b_spec = pl.BlockSpec((tk, LANES), lambda i: (i, 0))
    return pl.pallas_call(
        _dftmat_kernel,
        name="dftmat",
        grid=(rows // tk,),
        in_specs=[a_spec, a_spec, b_spec, b_spec],
        out_specs=pl.BlockSpec((2 * tk, n), lambda i: (i, 0)),
        out_shape=jax.ShapeDtypeStruct((2 * rows, n), BF),
        compiler_params=_params("parallel"),
    )(jnp.cos(ang_a), jnp.sin(ang_a), jnp.cos(ang_b), jnp.sin(ang_b))


def _twiddles(half, l2):
    i = np.arange(half, dtype=np.float64)
    rows = []
    for k in (i, l2 - i, l2 + i, 2 * l2 - i):
        ang = np.pi * k / (2 * l2)
        rows += [np.cos(ang), np.sin(ang)]
    tw = np.stack(rows).astype(np.float32)[:, :, None]
    return jnp.asarray(np.broadcast_to(tw, (8, half, LANES)).copy())


def _wide(t, n):
    return t if n == LANES else jnp.concatenate([t] * (n // LANES), axis=1)


def _spectrum_rows(pq, tw, tn):
    tk = pq.shape[0] // 2
    p0, p0a, p1, p1a = (pq[0:tk, j * tn:(j + 1) * tn] for j in range(4))
    q0, q0a, q1, q1a = (pq[tk:2 * tk, j * tn:(j + 1) * tn] for j in range(4))
    c1, s1, c2, s2, c3, s3, c4, s4 = tw
    return [(p0 + c1 * p1 - s1 * q1, -q0 - c1 * q1 - s1 * p1),
            (p0a + c2 * p1a + s2 * q1a, q0a + c2 * q1a - s2 * p1a),
            (p0a + c3 * p1a - s3 * q1a, -q0a - c3 * q1a - s3 * p1a),
            (p0 + c4 * p1 + s4 * q1, q0 + c4 * q1 - s4 * p1)]


_RT2 = math.sqrt(0.5)


def _mid_spectrum(pm0, qm0, pm1, qm1):
    zh = (pm0 + _RT2 * pm1 - _RT2 * qm1, -qm0 - _RT2 * qm1 - _RT2 * pm1)
    z3 = (pm0 - _RT2 * pm1 + _RT2 * qm1, qm0 - _RT2 * qm1 - _RT2 * pm1)
    return zh, z3


def _quarter_patterns(rows):
    m = lax.broadcasted_iota(jnp.int32, (rows, 1), 0) % 4
    cm = jnp.where(m == 0, 1.0, jnp.where(m == 2, -1.0, 0.0))
    sm = jnp.where(m == 1, 1.0, jnp.where(m == 3, -1.0, 0.0))
    return cm, sm, (1 - 2 * (m % 2)).astype(F32)


def _stage_inputs(x_ref, mid_ref, stage_ref, src_ref, pair):
    n, tn = src_ref.shape
    cm, sm, sign = _quarter_patterns(n // 2)
    for s in range(tn // LANES):
        lanes = slice(s * LANES, (s + 1) * LANES)
        stage_ref[s] = src_ref[:, lanes].astype(F32)
        for r in range(2):
            xf = stage_ref[s, pl.ds(r, n // 2, stride=2), :]
            col = (4 * pair + 2 * r) * tn + s * LANES
            x_ref[:, col:col + LANES] = xf.astype(BF)
            x_ref[:, col + tn:col + tn + LANES] = (xf * sign).astype(BF)
            row = 4 * pair + 2 * r
            mid_ref[row:row + 1, lanes] = jnp.sum(xf * cm, axis=0, keepdims=True)
            mid_ref[row + 1:row + 2, lanes] = jnp.sum(xf * sm, axis=0, keepdims=True)


def _filter_fft_kernel(hs_ref, hd_ref, m_ref, tw_ref, k_ref, kmid_ref, x_ref, mid_ref, stage_ref, *, l2):
    it = pl.program_id(1)
    tk = m_ref.shape[0] // 2
    tn = hs_ref.shape[1]

    @pl.when(it == 0)
    def _():
        _stage_inputs(x_ref, mid_ref, stage_ref, hs_ref, 0)
        _stage_inputs(x_ref, mid_ref, stage_ref, hd_ref, 1)
        zh_s, z3_s = _mid_spectrum(mid_ref[0:1, :], mid_ref[1:2, :], mid_ref[2:3, :], mid_ref[3:4, :])
        zh_d, z3_d = _mid_spectrum(mid_ref[4:5, :], mid_ref[5:6, :], mid_ref[6:7, :], mid_ref[7:8, :])
        scale = 0.5 / l2
        kmid_ref[0:1, :] = zh_s[0] * scale
        kmid_ref[1:2, :] = -zh_d[1] * scale
        kmid_ref[2:3, :] = z3_s[0] * scale
        kmid_ref[3:4, :] = -z3_d[1] * scale

    pq = _dot(m_ref[...], x_ref[...])
    tw = [_wide(tw_ref[j], tn) for j in range(8)]
    zs = _spectrum_rows(pq[:, 0:4 * tn], tw, tn)
    zd = _spectrum_rows(pq[:, 4 * tn:8 * tn], tw, tn)
    i = it * tk + lax.broadcasted_iota(jnp.int32, (tk, 1), 0)
    scale = jnp.where(i == 0, 0.25 / l2, 0.5 / l2)
    for j in range(4):
        k_ref[2 * j] = zs[j][0] * scale
        k_ref[2 * j + 1] = -zd[j][1] * scale


def _filter_fft_call(hs, hd, dft, tw, tk):
    n, wd = hs.shape
    l2 = n // 2
    half = l2 // 2
    tn = _tile(wd, 256, LANES)
    h_spec = pl.BlockSpec((n, tn), lambda c, i: (0, c))
    return pl.pallas_call(
        functools.partial(_filter_fft_kernel, l2=l2),
        name="filter_fft",
        grid=(wd // tn, half // tk),
        in_specs=[h_spec, h_spec,
                  pl.BlockSpec((2 * tk, l2), lambda c, i: (i, 0)),
                  pl.BlockSpec((8, tk, LANES), lambda c, i: (0, i, 0))],
        out_specs=[pl.BlockSpec((8, tk, tn), lambda c, i: (0, i, c)),
                   pl.BlockSpec((4, tn), lambda c, i: (0, c))],
        out_shape=[jax.ShapeDtypeStruct((8, half, wd), F32), jax.ShapeDtypeStruct((4, wd), F32)],
        scratch_shapes=[pltpu.VMEM((l2, 8 * tn), BF), pltpu.VMEM((8, tn), F32),
                        pltpu.VMEM((tn // LANES, n, LANES), F32)],
        compiler_params=_params("parallel", "arbitrary"),
    )(hs, hd, dft, tw)


def _fftconv_kernel(z_ref, m_ref, tw_ref, k_ref, kmid_ref, y_ref, x_ref, acc_ref, mid_ref, stage_ref):
    it = pl.program_id(2)
    n, tn = z_ref.shape
    l2 = n // 2

    @pl.when(it == 0)
    def _():
        _stage_inputs(x_ref, mid_ref, stage_ref, z_ref, 0)
        acc_ref[...] = jnp.zeros_like(acc_ref)

    m = m_ref[...]
    pq = _dot(m, x_ref[...])
    tw = [_wide(tw_ref[j], tn) for j in range(8)]
    zs = _spectrum_rows(pq, tw, tn)
    ys = [(zr * k_ref[2 * j] - zi * k_ref[2 * j + 1], zr * k_ref[2 * j + 1] + zi * k_ref[2 * j])
          for j, (zr, zi) in enumerate(zs)]
    (y1r, y1i), (y2r, y2i), (y3r, y3i), (y4r, y4i) = ys
    c1, s1, c2, s2 = tw[0], tw[1], tw[2], tw[3]
    dr, di = y1r - y4r, y1i + y4i
    er, ei = y2r - y3r, y2i + y3i
    blocks = [(y1r + y4r, -(y1i - y4i)),
              (y2r + y3r, y2i - y3i),
              (c1 * dr - s1 * di, -(c1 * di + s1 * dr)),
              (c2 * er - s2 * ei, c2 * ei + s2 * er)]
    g = jnp.concatenate([jnp.concatenate([a, b], axis=0) for a, b in blocks], axis=1).astype(BF)
    acc_ref[...] += lax.dot_general(m, g, _TN, preferred_element_type=F32)

    @pl.when(it == pl.num_programs(2) - 1)
    def _():
        cm, sm, sign = _quarter_patterns(l2)
        (zhr, zhi), (z3r, z3i) = _mid_spectrum(mid_ref[0:1, :], mid_ref[1:2, :], mid_ref[2:3, :], mid_ref[3:4, :])
        khr, khi, k3r, k3i = kmid_ref[0:1, :], kmid_ref[1:2, :], kmid_ref[2:3, :], kmid_ref[3:4, :]
        yhr, yhi = zhr * khr - zhi * khi, zhr * khi + zhi * khr
        ytr, yti = z3r * k3r - z3i * k3i, z3r * k3i + z3i * k3r
        fr, fi = yhr - ytr, yhi + yti
        g0r, g0i = yhr + ytr, yhi - yti
        g1r, g1i = _RT2 * (fr - fi), _RT2 * (fi + fr)
        ye = acc_ref[:, 0:tn] + sign * acc_ref[:, tn:2 * tn] + cm * g0r - sm * g0i
        yo = acc_ref[:, 2 * tn:3 * tn] + sign * acc_ref[:, 3 * tn:4 * tn] + cm * g1r - sm * g1i
        for s in range(tn // LANES):
            lanes = slice(s * LANES, (s + 1) * LANES)
            stage_ref[s, pl.ds(0, l2, stride=2), :] = ye[:, lanes]
            stage_ref[s, pl.ds(1, l2, stride=2), :] = yo[:, lanes]
            y_ref[:, lanes] = stage_ref[s].astype(y_ref.dtype)


def _fftconv_call(z, dft, tw, kf, kmid, batch, tk):
    t, wd = z.shape
    n = t // batch
    l2 = n // 2
    half = l2 // 2
    tn = _tile(wd, 256, LANES)
    z_spec = pl.BlockSpec((n, tn), lambda b, c, i: (b, c))
    return pl.pallas_call(
        _fftconv_kernel,
        name="fftconv",
        grid=(batch, wd // tn, half // tk),
        in_specs=[z_spec,
                  pl.BlockSpec((2 * tk, l2), lambda b, c, i: (i, 0)),
                  pl.BlockSpec((8, tk, LANES), lambda b, c, i: (0, i, 0)),
                  pl.BlockSpec((8, tk, tn), lambda b, c, i: (0, i, c)),
                  pl.BlockSpec((4, tn), lambda b, c, i: (0, c))],
        out_specs=z_spec,
        out_shape=jax.ShapeDtypeStruct((t, wd), BF),
        scratch_shapes=[pltpu.VMEM((l2, 4 * tn), BF), pltpu.VMEM((l2, 4 * tn), F32), pltpu.VMEM((8, tn), F32),
                        pltpu.VMEM((tn // LANES, n, LANES), F32)],
        compiler_params=_params("parallel", "parallel", "arbitrary"),
    )(z, dft, tw, kf, kmid)


def _mix_kernel(y_ref, x0_ref, z_ref, hbias_ref, of_ref, ob_ref, gz_ref, gn_ref, ghy_ref, ggdn_ref,
                why_ref, wgdn_ref, m_ref, *, dv, row_blocks):
    hbias, gn = hbias_ref[...], gn_ref[...]
    rb = m_ref.shape[0] // row_blocks
    ops = []
    for r in range(row_blocks):
        rs = slice(r * rb, (r + 1) * rb)
        z = z_ref[rs, :].astype(F32)
        uhy = (x0_ref[rs, :].astype(F32) * (y_ref[rs, :].astype(F32) + z * hbias)).astype(BF)
        parts = []
        for s in range(0, of_ref.shape[1], dv):
            o = of_ref[rs, s:s + dv].astype(F32) + ob_ref[rs, s:s + dv].astype(F32)
            on = o * lax.rsqrt(jnp.mean(o * o, axis=-1, keepdims=True) + EPS) * gn
            parts.append((on * gz_ref[rs, s:s + dv].astype(F32)).astype(BF))
        ops.append((rs, uhy, jnp.concatenate(parts, axis=1)))
    def epilogue(op, acc):
        rs = op[0]
        m_ref[rs, :] = (ghy_ref[rs, :].astype(F32) * acc[0] + ggdn_ref[rs, :].astype(F32) * acc[1]).astype(m_ref.dtype)

    _pipelined(ops, lambda op: (_dot(op[1], why_ref[...]), _dot(op[2], wgdn_ref[...])), epilogue)


def _mix_call(y, x0, z, hbias, o_f, o_b, gz, gn, gates, w_hy, w_gdn, dv):
    t, wd = y.shape
    hd = o_f.shape[1]
    d = w_hy.shape[1]
    tm = _tile(t, 256, 32)
    row = lambda c: pl.BlockSpec((tm, c), lambda i: (i, 0))
    return pl.pallas_call(
        functools.partial(_mix_kernel, dv=dv, row_blocks=2),
        name="mix",
        grid=(t // tm,),
        in_specs=[row(wd), row(wd), row(wd), pl.BlockSpec((1, wd), lambda i: (0, 0)),
                  row(hd), row(hd), row(hd), pl.BlockSpec((1, dv), lambda i: (0, 0)),
                  pl.BlockSpec((tm, d), lambda i: (i, 0)), pl.BlockSpec((tm, d), lambda i: (i, 1)),
                  _resident((wd, d)), _resident((hd, d))],
        out_specs=pl.BlockSpec((tm, d), lambda i: (i, 0)),
        out_shape=jax.ShapeDtypeStruct((t, d), BF),
        compiler_params=_params("parallel"),
    )(y, x0, z, hbias, o_f, o_b, gz, gn, gates, gates, w_hy, w_gdn)


def _resid_kernel(m_ref, w_ref, x_ref, ga_ref, o_ref):
    o_ref[...] = x_ref[...] + ga_ref[0] * _dot(m_ref[...], w_ref[...])


def _resid_call(m, w, x2, mod3, sec_ga, rows_per_mod):
    t, d = m.shape
    n = w.shape[1]
    tm = _tile(rows_per_mod, 512, 16)
    per = rows_per_mod // tm
    return pl.pallas_call(
        _resid_kernel,
        name="out_resid",
        grid=(t // tm,),
        in_specs=[pl.BlockSpec((tm, d), lambda i: (i, 0)),
                  _resident((d, n)),
                  pl.BlockSpec((tm, n), lambda i: (i, 0)),
                  pl.BlockSpec((1, 1, n), lambda i: (i // per, 0, sec_ga))],
        out_specs=pl.BlockSpec((tm, n), lambda i: (i, 0)),
        out_shape=jax.ShapeDtypeStruct((t, n), F32),
        compiler_params=_params("parallel"),
    )(m, w, x2, mod3)


def _ffn_up_kernel(x_ref, g_ref, sc_ref, sh_ref, wg_ref, wu_ref, o_ref, h_ref):
    @pl.when(pl.program_id(1) == 0)
    def _():
        x = x_ref[...]
        xn = x * lax.rsqrt(jnp.mean(x * x, axis=-1, keepdims=True) + EPS)
        h_ref[...] = ((xn * g_ref[...]) * (1.0 + sc_ref[0]) + sh_ref[0]).astype(BF)

    def matmul(b):
        h = h_ref[b[0], :]
        return _dot(h, wg_ref[:, b[1]]), _dot(h, wu_ref[:, b[1]])

    def epilogue(b, acc):
        o_ref[b[0], b[1]] = (_silu(acc[0]) * acc[1]).astype(o_ref.dtype)

    _pipelined(_sub_blocks(*o_ref.shape, _row_sub(o_ref.shape[0], 16, 4 * ROW_SUB)), matmul, epilogue)


def _ffn_up_call(x2, g, mod3, sec_sh, sec_sc, w_up, rows_per_mod):
    t, d = x2.shape
    ff = w_up.shape[1] // 2
    tm = _tile(rows_per_mod, ROW_TILE, 16)
    tn = _tile(ff, 512, LANES)
    nb = ff // tn
    per = rows_per_mod // tm
    mod_map = lambda sec: (lambda i, j: (i // per, 0, sec))
    return pl.pallas_call(
        _ffn_up_kernel,
        name="ffn_up",
        grid=(t // tm, nb),
        in_specs=[pl.BlockSpec((tm, d), lambda i, j: (i, 0)),
                  pl.BlockSpec((1, d), lambda i, j: (0, 0)),
                  pl.BlockSpec((1, 1, d), mod_map(sec_sc)),
                  pl.BlockSpec((1, 1, d), mod_map(sec_sh)),
                  pl.BlockSpec((d, tn), lambda i, j: (0, j)),
                  pl.BlockSpec((d, tn), lambda i, j: (0, nb + j))],
        out_specs=pl.BlockSpec((tm, tn), lambda i, j: (i, j)),
        out_shape=jax.ShapeDtypeStruct((t, ff), BF),
        scratch_shapes=[pltpu.VMEM((tm, d), BF)],
        compiler_params=_params("parallel", "arbitrary"),
    )(x2, g, mod3, mod3, w_up, w_up)


def _ffn_down_kernel(a_ref, w_ref, x_ref, ga_ref, gf_ref, o_ref):
    x = x_ref[...] + ga_ref[0] * _dot(a_ref[...], w_ref[...])
    o_ref[...] = x * lax.rsqrt(jnp.mean(x * x, axis=-1, keepdims=True) + EPS) * gf_ref[...]


def _ffn_down_call(a, w, x2, mod3, sec_ga, g_final, rows_per_mod):
    t, ff = a.shape
    d = w.shape[1]
    tm = _tile(rows_per_mod, 512, 16)
    per = rows_per_mod // tm
    return pl.pallas_call(
        _ffn_down_kernel,
        name="ffn_down",
        grid=(t // tm,),
        in_specs=[pl.BlockSpec((tm, ff), lambda i: (i, 0)),
                  _resident((ff, d)),
                  pl.BlockSpec((tm, d), lambda i: (i, 0)),
                  pl.BlockSpec((1, 1, d), lambda i: (i // per, 0, sec_ga)),
                  pl.BlockSpec((1, d), lambda i: (0, 0))],
        out_specs=pl.BlockSpec((tm, d), lambda i: (i, 0)),
        out_shape=jax.ShapeDtypeStruct((t, d), F32),
        compiler_params=_params("parallel"),
    )(a, w, x2, mod3, g_final)


def _group_columns(a, heads, hb):
    g = heads // hb
    lead = a.shape[:-1]
    a = a.reshape(*lead, 4, g, hb)
    a = jnp.moveaxis(a, -2, -3).reshape(*lead, g, 4 * hb)
    a = jnp.pad(a, [(0, 0)] * (len(lead) + 1) + [(0, LANES - 4 * hb)])
    return a.reshape(*lead, g * LANES)


def _layer(x, c, ctx, c_ctx, w_ada, b_ada, norm_mix, norm_ffn, w_in, hy_conv, hy_bias,
           hy_fw1, hy_fb1, hy_fw2, hy_fb2, hy_fw3, hy_fb3, hy_fout, hy_freq,
           gdn_conv, gdn_a_log, gdn_dt_bias, gdn_norm, w_hy_out, w_gdn_out, w_o, w_up, w_down,
           norm_final):
    b, n, d = x.shape
    n_ctx = ctx.shape[1]
    wd = hy_bias.shape[-1]
    heads = gdn_a_log.shape[-1]
    dv = gdn_norm.shape[-1]
    d_v = heads * dv
    d_qk = (gdn_conv.shape[-1] - d_v) // 2
    dk = d_qk // heads
    hb = min(16, heads)
    groups = heads // hb
    i_hy, i_qkv, i_z, i_sc = 3 * wd, 3 * wd + 2 * d_qk + d_v, 3 * wd + 2 * d_qk + 2 * d_v, 3 * wd + 2 * d_qk + 2 * d_v + 4 * heads

    rows = -(-(b + 1) // 8) * 8
    cvec = jnp.concatenate([jax.nn.silu(c), jax.nn.silu(c_ctx)[None], jnp.zeros((rows - b - 1, d), F32)], axis=0)
    mod = _ada_call(cvec.astype(BF), w_ada, b_ada.reshape(1, -1))
    mod3 = mod.reshape(rows, 1, 6 * d)

    x2 = x.reshape(b * n, d)
    ctx2 = ctx.reshape(b * n_ctx, d)
    g_mix = norm_mix.reshape(1, d)
    h_lat = _normmod_call(x2, g_mix, mod3, 0, 1, n, 0)
    h_ctx = _normmod_call(ctx2, g_mix, mod3, 0, 1, b * n_ctx, b)

    w_in16 = w_in.astype(BF)
    w_gate = w_in16[:, i_sc:]
    w_scal = _group_columns(w_in[:, i_z:i_sc], heads, hb).astype(BF)
    a_vec = _group_columns(jnp.concatenate([jnp.zeros((2 * heads,), F32), gdn_a_log.reshape(-1)]), heads, hb)[None]
    dt_vec = _group_columns(jnp.concatenate([jnp.zeros((2 * heads,), F32), gdn_dt_bias.reshape(-1)]), heads, hb)[None]

    def gdn_branch(h, period, s0f, s0b):
        t = h.shape[0]
        qk = _proj_qkv_call(h, w_in16, i_hy, gdn_conv, 0, 2 * d_qk, period, dk)
        v = _proj_qkv_call(h, w_in16, i_hy + 2 * d_qk, gdn_conv, 2 * d_qk, d_v, period, 0)
        scal = _scal_call(h, w_scal, a_vec, dt_vec, hb)
        scal_t = scal.reshape(t // CHUNK, CHUNK, groups, LANES).transpose(0, 2, 3, 1)
        return _gdn_call(qk, v, scal, scal_t, s0f, s0b, b, heads, dk, dv, hb)

    zero_state = jnp.zeros((b, heads, dk, dv), F32)
    _, _, s_f, s_b = gdn_branch(h_ctx, n_ctx, zero_state, zero_state)
    o_f, o_b, _, _ = gdn_branch(h_lat, GRID_W, s_f, s_b)

    x0, z = _proj_hy_call(h_lat, w_in16, 0, wd, hy_conv, GRID_W)
    tt = np.linspace(0.0, 1.0, n, dtype=np.float32)[:, None]
    bands = (hy_fw1.shape[0] - 1) // 2
    wv = (2.0 * math.pi * np.arange(n, dtype=np.float32)[:, None] / n).astype(np.float32)
    fv = np.linspace(1e-4, bands - 1, bands, dtype=np.float32)[None, :]
    zfeat = np.concatenate([tt, np.cos(fv * wv), -np.sin(fv * wv)], axis=-1).astype(np.float32)
    emb = zfeat.shape[1]
    emb_pad = -(-emb // 8) * 8
    zfeat = jnp.asarray(np.pad(zfeat, [(0, 0), (0, emb_pad - emb)]))
    fw1 = jnp.pad(hy_fw1, [(0, emb_pad - emb), (0, 0)])
    deltas = jnp.abs(jnp.linspace(HY_MIN_DECAY, HY_MAX_DECAY, wd, dtype=F32))[None]
    hs, hdiff = _filter_call(zfeat, fw1, hy_fb1[None], hy_fw2, hy_fb2[None], hy_fw3, hy_fb3[None],
                             hy_freq[None], hy_fout, jnp.asarray(tt), deltas)
    l2 = n // 2
    tk = _tile(l2 // 2, 256, 16)
    dft = _dftmat_call(l2, l2 // 2, tk)
    tw = _twiddles(l2 // 2, l2)
    kf, kmid = _filter_fft_call(hs, hdiff, dft, tw, tk)
    y = _fftconv_call(z, dft, tw, kf, kmid, b, tk)

    gz = _proj_act_call(h_lat, w_in16, i_qkv, d_v, _silu)
    gates = _proj_act_call(h_lat, w_gate, 0, 2 * d, _sigmoid)
    m = _mix_call(y, x0, z, hy_bias[None], o_f, o_b, gz, gdn_norm[None], gates,
                  w_hy_out.astype(BF), w_gdn_out.astype(BF), dv)
    x1 = _resid_call(m, w_o.astype(BF), x2, mod3, 2, n)

    a = _ffn_up_call(x1, norm_ffn.reshape(1, d), mod3, 3, 4, w_up.astype(BF), n)
    out = _ffn_down_call(a, w_down.astype(BF), x1, mod3, 5, norm_final.reshape(1, d), n)
    return out.reshape(b, n, d)


def kernel(x, c, ctx, c_ctx, w_ada, b_ada, norm_mix, norm_ffn, w_in, hy_conv, hy_bias, hy_fw1, hy_fb1, hy_fw2, hy_fb2, hy_fw3, hy_fb3, hy_fout, hy_freq, gdn_conv, gdn_a_log, gdn_dt_bias, gdn_norm, w_hy_out, w_gdn_out, w_o, w_up, w_down, norm_final):
    assert w_ada.shape[0] == 1, "single-layer block: the context stream is only read"
    return _layer(x, c, ctx, c_ctx, w_ada[0], b_ada[0], norm_mix[0], norm_ffn[0], w_in[0], hy_conv[0],
                  hy_bias[0], hy_fw1[0], hy_fb1[0], hy_fw2[0], hy_fb2[0], hy_fw3[0], hy_fb3[0],
                  hy_fout[0], hy_freq[0], gdn_conv[0], gdn_a_log[0], gdn_dt_bias[0], gdn_norm[0],
                  w_hy_out[0], w_gdn_out[0], w_o[0], w_up[0], w_down[0], norm_final)
```
